```python
import jax, jax.numpy as jnp
from jax import lax
import numpy as np

D_MODEL = 2048
BATCH = 1
SEQ = 16384
DEPTH = 2

N_META = 16
NORM_EPS = 1e-6
LRU_WIDTH = D_MODEL // 2
LRU_BLOCKS = 16
LRU_BLOCK = LRU_WIDTH // LRU_BLOCKS
LRU_CONV = 4
LRU_C = 8.0
RWKV_HEADS = 16
RWKV_HEAD_DIM = 64
RWKV_WIDTH = RWKV_HEADS * RWKV_HEAD_DIM
RWKV_W_RANK = 64
RWKV_A_RANK = 64
RWKV_G_RANK = 128
RWKV_GN_EPS = 64e-5
RWKV_IN = 3 * RWKV_WIDTH + RWKV_W_RANK + RWKV_A_RANK + RWKV_G_RANK
AB_IN = 2 * LRU_WIDTH + RWKV_IN
AB_MIX = LRU_WIDTH + RWKV_WIDTH
GLA_HEADS = 4
GLA_DK = D_MODEL // 2 // GLA_HEADS
GLA_DV = D_MODEL // GLA_HEADS
GLA_KW = GLA_HEADS * GLA_DK
GLA_VW = GLA_HEADS * GLA_DV
GLA_GATE_RANK = 16
GLA_GATE_NORM = 16.0
GLA_CHUNK = 64
GLA_IN = 2 * GLA_KW + 2 * GLA_VW + GLA_GATE_RANK
D_FF = 5632
FFN_CONV = 3
N_EVEN = (DEPTH + 1) // 2
N_ODD = DEPTH // 2

kernel_name = 'hybrid_rglru_rwkv7_gla_convffn'


def rmsnorm(x, w):
    xf = x.astype(jnp.float32)
    y = xf * lax.rsqrt(jnp.mean(xf * xf, -1, keepdims=True) + NORM_EPS)
    return (y * w.astype(jnp.float32)).astype(x.dtype)


def causal_dwconv(x, w, b):
    K, C = w.shape
    y = lax.conv_general_dilated(x, w[:, None, :].astype(x.dtype), window_strides=(1,),
                                 padding=[(K - 1, 0)], dimension_numbers=('NWC', 'WIO', 'NWC'),
                                 feature_group_count=C)
    return y + b.astype(x.dtype)


def token_shift(x):
    return jnp.pad(x, ((0, 0), (1, 0), (0, 0)))[:, :-1]


def rglru(xc, ga_w, ga_b, gx_w, gx_b, lam):
    B, T, _ = xc.shape
    xf = xc.astype(jnp.float32)
    xb = xf.reshape(B, T, LRU_BLOCKS, LRU_BLOCK)
    r = jax.nn.sigmoid(jnp.einsum('bthi,hij->bthj', xb, ga_w).reshape(B, T, LRU_WIDTH) + ga_b)
    i = jax.nn.sigmoid(jnp.einsum('bthi,hij->bthj', xb, gx_w).reshape(B, T, LRU_WIDTH) + gx_b)
    log_a = -LRU_C * r * jax.nn.softplus(-lam)
    a = jnp.exp(log_a)
    u = jnp.sqrt(-jnp.expm1(2.0 * log_a)) * (i * xf)

    def combine(left, right):
        a1, b1 = left
        a2, b2 = right
        return a1 * a2, a2 * b1 + b2

    _, h = lax.associative_scan(combine, (a, u), axis=1)
    return h.astype(xc.dtype)


def rwkv7(u_in, shift_mu, w0, w2, a0, a2, g2, k_k, k_a, r_k, ln_w, ln_b):
    B, T, _ = u_in.shape
    u = u_in.astype(jnp.float32)
    u = u + (token_shift(u) - u) * shift_mu
    o1 = RWKV_WIDTH
    o2 = 2 * RWKV_WIDTH
    o3 = 3 * RWKV_WIDTH
    o4 = o3 + RWKV_W_RANK
    o5 = o4 + RWKV_A_RANK
    r, k, v, wd, ad, gd = jnp.split(u, [o1, o2, o3, o4, o5], axis=-1)
    w = -jax.nn.softplus(-(w0 + jnp.tanh(wd) @ w2)) - 0.5
    decay = jnp.exp(-jnp.exp(w))
    a = jax.nn.sigmoid(a0 + ad @ a2)
    g = jax.nn.sigmoid(gd) @ g2
    heads = lambda t: t.reshape(B, T, RWKV_HEADS, RWKV_HEAD_DIM)
    kk = heads(k * k_k)
    kk = kk / jnp.maximum(jnp.sqrt(jnp.sum(kk * kk, -1, keepdims=True)), 1e-12)
    k = k * (1.0 + (a - 1.0) * k_a)
    rh, kh, vh, dh, ah = heads(r), heads(k), heads(v), heads(decay), heads(a)

    def step(S, inp):
        r_t, w_t, k_t, v_t, kk_t, a_t = inp
        s_kk = jnp.einsum('bhvk,bhk->bhv', S, kk_t)
        S = (S * w_t[:, :, None, :] - s_kk[..., None] * (kk_t * a_t)[:, :, None, :]
             + v_t[..., None] * k_t[:, :, None, :])
        return S, jnp.einsum('bhvk,bhk->bhv', S, r_t)

    xs = tuple(jnp.moveaxis(t, 1, 0) for t in (rh, dh, kh, vh, kk, ah))
    S0 = jnp.zeros((B, RWKV_HEADS, RWKV_HEAD_DIM, RWKV_HEAD_DIM), jnp.float32)
    _, y = lax.scan(step, S0, xs)
    y = jnp.moveaxis(y, 0, 1)
    mu = jnp.mean(y, -1, keepdims=True)
    var = jnp.mean(jnp.square(y - mu), -1, keepdims=True)
    y = ((y - mu) * lax.rsqrt(var + RWKV_GN_EPS)).reshape(B, T, RWKV_WIDTH) * ln_w + ln_b
    bonus = (jnp.sum(rh * kh * r_k, -1, keepdims=True) * vh).reshape(B, T, RWKV_WIDTH)
    return ((y + bonus) * g).astype(u_in.dtype)


def hybrid_ab(h, w_in, lru_conv_w, lru_conv_b, ga_w, ga_b, gx_w, gx_b, lam,
              shift_mu, w0, w2, a0, a2, g2, k_k, k_a, r_k, ln_w, ln_b, w_out):
    u = h @ w_in
    lx = u[..., :LRU_WIDTH]
    ly = u[..., LRU_WIDTH:2 * LRU_WIDTH]
    ur = u[..., 2 * LRU_WIDTH:]
    lx = causal_dwconv(lx, lru_conv_w, lru_conv_b)
    lru_out = rglru(lx, ga_w, ga_b, gx_w, gx_b, lam) * jax.nn.gelu(ly)
    rw_out = rwkv7(ur, shift_mu, w0, w2, a0, a2, g2, k_k, k_a, r_k, ln_w, ln_b)
    return jnp.concatenate([lru_out, rw_out], -1) @ w_out


def gla_mixer(h, w_in, gk_up, gk_b, norm_w, w_out):
    B, T, _ = h.shape
    u = (h @ w_in).astype(jnp.float32)
    q, k, v, g, gd = jnp.split(u, [GLA_KW, 2 * GLA_KW, 2 * GLA_KW + GLA_VW, 2 * GLA_KW + 2 * GLA_VW], axis=-1)
    gk = jax.nn.log_sigmoid(gd @ gk_up + gk_b) / GLA_GATE_NORM
    q = q * (GLA_DK ** -0.5)
    pad = (-N_META) % GLA_CHUNK
    L = T + pad
    NC = L // GLA_CHUNK

    def to_chunks(t, d):
        t = jnp.pad(t.reshape(B, T, GLA_HEADS, d), ((0, 0), (pad, 0), (0, 0), (0, 0)))
        return t.reshape(B, NC, GLA_CHUNK, GLA_HEADS, d).transpose(1, 0, 3, 2, 4)

    qc, kc, gc = to_chunks(q, GLA_DK), to_chunks(k, GLA_DK), to_chunks(gk, GLA_DK)
    vc = to_chunks(v, GLA_DV)
    mask = jnp.tril(jnp.ones((GLA_CHUNK, GLA_CHUNK), bool))

    def step(S, inp):
        q_c, k_c, v_c, g_c = inp
        b = jnp.cumsum(g_c, axis=2)
        o_inter = jnp.einsum('bhcd,bhde->bhce', q_c * jnp.exp(b), S)
        diff = b[:, :, :, None, :] - b[:, :, None, :, :]
        dec = jnp.exp(jnp.where(mask[:, :, None], diff, -jnp.inf))
        A = jnp.sum(q_c[:, :, :, None, :] * k_c[:, :, None, :, :] * dec, -1)
        o = o_inter + jnp.einsum('bhij,bhje->bhie', A, v_c)
        b_last = b[:, :, -1:, :]
        S = S * jnp.exp(b_last[:, :, 0, :])[..., None] + jnp.einsum('bhcd,bhce->bhde', k_c * jnp.exp(b_last - b), v_c)
        return S, o

    S0 = jnp.zeros((B, GLA_HEADS, GLA_DK, GLA_DV), jnp.float32)
    _, o = lax.scan(step, S0, (qc, kc, vc, gc))
    o = o.transpose(1, 0, 3, 2, 4).reshape(B, L, GLA_HEADS, GLA_DV)[:, pad:]
    o = rmsnorm(o, norm_w).reshape(B, T, GLA_VW) * jax.nn.silu(g)
    return o.astype(h.dtype) @ w_out


def conv_ffn(h, w_gate, w_val, conv_w, conv_b, w_down):
    gate = causal_dwconv(h @ w_gate, conv_w, conv_b)
    return (jax.nn.gelu(gate) * (h @ w_val)) @ w_down


def setup_inputs(seed: int = 0) -> dict:
    key = jax.random.key(seed)
    ks = iter(jax.random.split(key, 48))
    f32 = jnp.float32

    def nrm(shape, scale):
        return jax.random.normal(next(ks), shape, f32) * scale

    def gain(shape):
        return 1.0 + 0.02 * jax.random.normal(next(ks), shape, f32)

    s = jax.random.uniform(next(ks), (N_EVEN, LRU_WIDTH), f32, 0.9, 0.999) ** (1.0 / LRU_C)
    return {
        'x': nrm((BATCH, SEQ, D_MODEL), 1.0),
        'meta_tokens': nrm((N_META, D_MODEL), 1.0),
        'mix_pre_norm': gain((DEPTH, D_MODEL)),
        'mix_post_norm': gain((DEPTH, D_MODEL)),
        'ffn_pre_norm': gain((DEPTH, D_MODEL)),
        'ffn_post_norm': gain((DEPTH, D_MODEL)),
        'ab_w_in': nrm((N_EVEN, D_MODEL, AB_IN), D_MODEL ** -0.5),
        'lru_conv_w': nrm((N_EVEN, LRU_CONV, LRU_WIDTH), LRU_CONV ** -0.5),
        'lru_conv_b': nrm((N_EVEN, LRU_WIDTH), 0.02),
        'lru_gate_a_w': nrm((N_EVEN, LRU_BLOCKS, LRU_BLOCK, LRU_BLOCK), LRU_BLOCK ** -0.5),
        'lru_gate_a_b': nrm((N_EVEN, LRU_WIDTH), 0.02),
        'lru_gate_x_w': nrm((N_EVEN, LRU_BLOCKS, LRU_BLOCK, LRU_BLOCK), LRU_BLOCK ** -0.5),
        'lru_gate_x_b': nrm((N_EVEN, LRU_WIDTH), 0.02),
        'lru_lambda': jnp.log(s) - jnp.log1p(-s),
        'rwkv_shift_mu': jax.random.uniform(next(ks), (N_EVEN, RWKV_IN), f32),
        'rwkv_w0': jax.random.uniform(next(ks), (N_EVEN, RWKV_WIDTH), f32, -6.0, 1.0),
        'rwkv_w2': nrm((N_EVEN, RWKV_W_RANK, RWKV_WIDTH), 0.5 * RWKV_W_RANK ** -0.5),
        'rwkv_a0': nrm((N_EVEN, RWKV_WIDTH), 0.1),
        'rwkv_a2': nrm((N_EVEN, RWKV_A_RANK, RWKV_WIDTH), 0.5 * RWKV_A_RANK ** -0.5),
        'rwkv_g2': nrm((N_EVEN, RWKV_G_RANK, RWKV_WIDTH), RWKV_G_RANK ** -0.5),
        'rwkv_k_k': 0.85 + nrm((N_EVEN, RWKV_WIDTH), 0.05),
        'rwkv_k_a': gain((N_EVEN, RWKV_WIDTH)),
        'rwkv_r_k': nrm((N_EVEN, RWKV_HEADS, RWKV_HEAD_DIM), 0.1),
        'rwkv_ln_w': gain((N_EVEN, RWKV_WIDTH)),
        'rwkv_ln_b': nrm((N_EVEN, RWKV_WIDTH), 0.02),
        'ab_w_out': nrm((N_EVEN, AB_MIX, D_MODEL), AB_MIX ** -0.5),
        'gla_w_in': nrm((N_ODD, D_MODEL, GLA_IN), D_MODEL ** -0.5),
        'gla_gk_up': nrm((N_ODD, GLA_GATE_RANK, GLA_KW), GLA_GATE_RANK ** -0.5),
        'gla_gk_b': nrm((N_ODD, GLA_KW), 0.1),
        'gla_norm_w': gain((N_ODD, GLA_DV)),
        'gla_w_out': nrm((N_ODD, GLA_VW, D_MODEL), GLA_VW ** -0.5),
        'ffn_w_gate': nrm((DEPTH, D_MODEL, D_FF), D_MODEL ** -0.5),
        'ffn_w_val': nrm((DEPTH, D_MODEL, D_FF), D_MODEL ** -0.5),
        'ffn_conv_w': nrm((DEPTH, FFN_CONV, D_FF), FFN_CONV ** -0.5),
        'ffn_conv_b': nrm((DEPTH, D_FF), 0.02),
        'ffn_w_down': nrm((DEPTH, D_FF, D_MODEL), D_FF ** -0.5),
    }


def reference(x, meta_tokens, mix_pre_norm, mix_post_norm, ffn_pre_norm, ffn_post_norm,
              ab_w_in, lru_conv_w, lru_conv_b, lru_gate_a_w, lru_gate_a_b, lru_gate_x_w, lru_gate_x_b,
              lru_lambda, rwkv_shift_mu, rwkv_w0, rwkv_w2, rwkv_a0, rwkv_a2, rwkv_g2, rwkv_k_k, rwkv_k_a,
              rwkv_r_k, rwkv_ln_w, rwkv_ln_b, ab_w_out, gla_w_in, gla_gk_up, gla_gk_b, gla_norm_w,
              gla_w_out, ffn_w_gate, ffn_w_val, ffn_conv_w, ffn_conv_b, ffn_w_down):
    B = x.shape[0]
    meta = jnp.broadcast_to(meta_tokens[None].astype(x.dtype), (B, N_META, D_MODEL))
    h = jnp.concatenate([meta, x], axis=1)
    for layer in range(DEPTH):
        j = layer // 2
        hn = rmsnorm(h, mix_pre_norm[layer])
        if layer % 2 == 0:
            m = hybrid_ab(hn, ab_w_in[j], lru_conv_w[j], lru_conv_b[j], lru_gate_a_w[j], lru_gate_a_b[j],
                          lru_gate_x_w[j], lru_gate_x_b[j], lru_lambda[j], rwkv_shift_mu[j], rwkv_w0[j],
                          rwkv_w2[j], rwkv_a0[j], rwkv_a2[j], rwkv_g2[j], rwkv_k_k[j], rwkv_k_a[j],
                          rwkv_r_k[j], rwkv_ln_w[j], rwkv_ln_b[j], ab_w_out[j])
        else:
            m = gla_mixer(hn, gla_w_in[j], gla_gk_up[j], gla_gk_b[j], gla_norm_w[j], gla_w_out[j])
        h = h + rmsnorm(m, mix_post_norm[layer])
        f = conv_ffn(rmsnorm(h, ffn_pre_norm[layer]), ffn_w_gate[layer], ffn_w_val[layer],
                     ffn_conv_w[layer], ffn_conv_b[layer], ffn_w_down[layer])
        h = h + rmsnorm(f, ffn_post_norm[layer])
    return h[:, N_META:]
```

```python
import functools
import math

import jax
import jax.numpy as jnp
from jax import lax
from jax.experimental import pallas as pl
from jax.experimental.pallas import tpu as pltpu

F32 = jnp.float32
BF16 = jnp.bfloat16

D_MODEL = 2048
N_META = 16
NORM_EPS = 1e-6
HEAD = 256
PAD = HEAD - N_META
LRU_WIDTH = 1024
LRU_BLOCK = 64
LRU_GROUP = 256
LRU_CONV = 4
LRU_C = 8.0
RWKV_WIDTH = 1024
RWKV_HEAD_DIM = 64
RWKV_LORA = 128
RWKV_IN = 3 * RWKV_WIDTH + 2 * RWKV_LORA
RWKV_GN_EPS = 64e-5
RWKV_CHUNK = 64
RWKV_GROUP = 256
AB_IN = 2 * LRU_WIDTH + RWKV_IN
GLA_HEADS = 4
GLA_DK = 256
GLA_DV = 512
GLA_KW = GLA_HEADS * GLA_DK
GLA_VW = GLA_HEADS * GLA_DV
GLA_GATE_RANK = 16
GLA_GATE_NORM = 16.0
GLA_CHUNK = 64
GLA_SUB = 16
GLA_MAIN = 2 * GLA_KW + 2 * GLA_VW
GLA_IN_PAD = GLA_MAIN + 128
D_FF = 5632
FFN_CONV = 3
VMEM_LIMIT = 56 * 1024 * 1024


def _pick(n, candidates):
    for c in candidates:
        if n % c == 0:
            return c
    raise ValueError(f"no tile for {n} in {candidates}")


def _params(sem):
    return pltpu.CompilerParams(dimension_semantics=sem, vmem_limit_bytes=VMEM_LIMIT)


def _dot(a, b):
    return jnp.dot(a.astype(BF16), b.astype(BF16), preferred_element_type=F32)


def _dot_nt(a, b):
    return lax.dot_general(a.astype(BF16), b.astype(BF16), (((1,), (1,)), ((), ())),
                           preferred_element_type=F32)


def _dot_tn(a, b):
    return lax.dot_general(a.astype(BF16), b.astype(BF16), (((0,), (0,)), ((), ())),
                           preferred_element_type=F32)


def _dot_split(a, b_bf16):
    hi = a.astype(BF16)
    lo = (a - hi.astype(F32)).astype(BF16)
    return (jnp.dot(hi, b_bf16, preferred_element_type=F32)
            + jnp.dot(lo, b_bf16, preferred_element_type=F32))


def _softplus(x):
    return jnp.maximum(x, 0.0) + jnp.log1p(jnp.exp(-jnp.abs(x)))


def _expm1(x):
    u = jnp.exp(x)
    d = u - 1.0
    return jnp.where(d == 0.0, x, d * x / jnp.log(jnp.where(d == 0.0, 2.0, u)))


def _rms(x):
    return x * lax.rsqrt(jnp.mean(x * x, axis=-1, keepdims=True) + NORM_EPS)


def _row_ids(shape, base):
    return lax.broadcasted_iota(jnp.int32, shape, 0) + base


def _shift_rows(x, s, prev8):
    rolled = pltpu.roll(x, s, 0)
    head = pltpu.roll(prev8, s, 0)
    rows = lax.broadcasted_iota(jnp.int32, x.shape, 0)
    return jnp.where(rows < s, jnp.tile(head, (x.shape[0] // 8, 1)), rolled)


def _embed_kernel(head_ref, x_ref, g_ref, h_ref, hn_ref):
    i = pl.program_id(0)
    xb = jnp.where(i == 0, head_ref[...], x_ref[...])
    h_ref[...] = xb
    hn_ref[...] = (_rms(xb) * g_ref[...]).astype(BF16)


def _embed(head, x2d, gain):
    seq = x2d.shape[0]
    tp = seq + HEAD
    blk = pl.BlockSpec((HEAD, D_MODEL), lambda i: (i, 0))
    return pl.pallas_call(
        _embed_kernel,
        grid=(tp // HEAD,),
        in_specs=[pl.BlockSpec((HEAD, D_MODEL), lambda i: (0, 0)),
                  pl.BlockSpec((HEAD, D_MODEL), lambda i: (jnp.maximum(i - 1, 0), 0)),
                  pl.BlockSpec((1, D_MODEL), lambda i: (0, 0))],
        out_specs=[blk, blk],
        out_shape=[jax.ShapeDtypeStruct((tp, D_MODEL), F32),
                   jax.ShapeDtypeStruct((tp, D_MODEL), BF16)],
        compiler_params=_params(("arbitrary",)),
        name="embed",
    )(head, x2d, gain)


def _mm_kernel(x_ref, w_ref, o_ref):
    o_ref[...] = jnp.dot(x_ref[...], w_ref[...], preferred_element_type=F32).astype(o_ref.dtype)


def _mm(x, w, tn, name):
    tp, k = x.shape
    n = w.shape[1]
    tm = _pick(tp, (1280, 768, 256))
    return pl.pallas_call(
        _mm_kernel,
        grid=(tp // tm, n // tn),
        in_specs=[pl.BlockSpec((tm, k), lambda i, j: (i, 0)),
                  pl.BlockSpec((k, tn), lambda i, j: (0, j))],
        out_specs=pl.BlockSpec((tm, tn), lambda i, j: (i, j)),
        out_shape=jax.ShapeDtypeStruct((tp, n), F32),
        compiler_params=_params(("arbitrary", "arbitrary")),
        name=name,
    )(x, w)


def _mm_res_kernel(*refs, n_x, tm, has_next):
    x_refs = refs[:n_x]
    w_refs = refs[n_x:2 * n_x]
    h_ref, gpost_ref = refs[2 * n_x:2 * n_x + 2]
    rest = refs[2 * n_x + 2:]
    if has_next:
        gnext_ref, ho_ref, hn_ref, acc_ref = rest
    else:
        ho_ref, acc_ref = rest
    i = pl.program_id(0)
    k = pl.program_id(1)
    part = jnp.dot(x_refs[0][...], w_refs[0][...], preferred_element_type=F32)
    for xr, wr in zip(x_refs[1:], w_refs[1:]):
        part = part + jnp.dot(xr[...], wr[...], preferred_element_type=F32)

    @pl.when(k == 0)
    def _():
        acc_ref[...] = part

    @pl.when(k != 0)
    def _():
        acc_ref[...] += part

    @pl.when(k == pl.num_programs(1) - 1)
    def _():
        m = acc_ref[...]
        upd = _rms(m) * gpost_ref[...]
        valid = _row_ids(m.shape, i * tm) >= PAD
        hnew = h_ref[...] + jnp.where(valid, upd, 0.0)
        ho_ref[...] = hnew
        if has_next:
            hn_ref[...] = (_rms(hnew) * gnext_ref[...]).astype(BF16)


def _mm_res(xs, ws, h, gpost, gnext, tk, name):
    tp = h.shape[0]
    kdim = xs[0].shape[1]
    tm = _pick(tp, (640, 256))
    has_next = gnext is not None
    row = pl.BlockSpec((tm, D_MODEL), lambda i, k: (i, 0))
    gain = pl.BlockSpec((1, D_MODEL), lambda i, k: (0, 0))
    in_specs = ([pl.BlockSpec((tm, tk), lambda i, k: (i, k)) for _ in xs]
                + [pl.BlockSpec((tk, D_MODEL), lambda i, k: (k, 0)) for _ in ws]
                + [row, gain] + ([gain] if has_next else []))
    out_specs = [row] + ([row] if has_next else [])
    out_shape = [jax.ShapeDtypeStruct((tp, D_MODEL), F32)]
    if has_next:
        out_shape.append(jax.ShapeDtypeStruct((tp, D_MODEL), BF16))
    args = list(xs) + list(ws) + [h, gpost] + ([gnext] if has_next else [])
    out = pl.pallas_call(
        functools.partial(_mm_res_kernel, n_x=len(xs), tm=tm, has_next=has_next),
        grid=(tp // tm, kdim // tk),
        in_specs=in_specs,
        out_specs=out_specs,
        out_shape=out_shape,
        scratch_shapes=[pltpu.VMEM((tm, D_MODEL), F32)],
        compiler_params=_params(("arbitrary", "arbitrary")),
        name=name,
    )(*args)
    return out if has_next else (out[0], None)


def _ffn_up_kernel(x_ref, xh_ref, wg_ref, wv_ref, cw_ref, cb_ref, o_ref):
    x = x_ref[...]
    gate = jnp.dot(x, wg_ref[...], preferred_element_type=F32)
    val = jnp.dot(x, wv_ref[...], preferred_element_type=F32)
    halo = jnp.dot(xh_ref[...], wg_ref[...], preferred_element_type=F32)[8:16]
    cw = cw_ref[...]
    conv = (cw[2:3] * gate + cw[1:2] * _shift_rows(gate, 1, halo)
            + cw[0:1] * _shift_rows(gate, 2, halo) + cb_ref[...])
    o_ref[...] = (jax.nn.gelu(conv) * val).astype(BF16)


def _ffn_up(hn, wg, wv, conv_w, conv_b):
    tp = hn.shape[0]
    tm = _pick(tp, (1280, 768, 256))
    tn = 512
    halo_blocks = tm // 16
    return pl.pallas_call(
        _ffn_up_kernel,
        grid=(tp // tm, D_FF // tn),
        in_specs=[pl.BlockSpec((tm, D_MODEL), lambda i, j: (i, 0)),
                  pl.BlockSpec((16, D_MODEL), lambda i, j: (jnp.maximum(i * halo_blocks - 1, 0), 0)),
                  pl.BlockSpec((D_MODEL, tn), lambda i, j: (0, j)),
                  pl.BlockSpec((D_MODEL, tn), lambda i, j: (0, j)),
                  pl.BlockSpec((8, tn), lambda i, j: (0, j)),
                  pl.BlockSpec((1, tn), lambda i, j: (0, j))],
        out_specs=pl.BlockSpec((tm, tn), lambda i, j: (i, j)),
        out_shape=jax.ShapeDtypeStruct((tp, D_FF), BF16),
        compiler_params=_params(("arbitrary", "arbitrary")),
        name="ffn_up",
    )(hn, hn, wg, wv, conv_w, conv_b)


def _lru_kernel(lx_ref, ly_ref, cw_ref, cb_ref, wa_ref, ba_ref, wx_ref, bx_ref, lam_ref,
                o_ref, halo_ref, state_ref, *, tt):
    i = pl.program_id(0)

    @pl.when(i == 0)
    def _():
        halo_ref[...] = jnp.zeros_like(halo_ref)
        state_ref[...] = jnp.zeros_like(state_ref)

    lx = lx_ref[...]
    prev = halo_ref[...]
    cw = cw_ref[...]
    xc = (cw[3:4] * lx + cw[2:3] * _shift_rows(lx, 1, prev) + cw[1:2] * _shift_rows(lx, 2, prev)
          + cw[0:1] * _shift_rows(lx, 3, prev) + cb_ref[...])
    halo_ref[...] = lx[tt - 8:tt]

    xb = xc.astype(BF16)
    ga, gx = [], []
    for g in range(LRU_WIDTH // LRU_GROUP):
        blk = xb[:, g * LRU_GROUP:(g + 1) * LRU_GROUP]
        ga.append(jnp.dot(blk, wa_ref[g], preferred_element_type=F32))
        gx.append(jnp.dot(blk, wx_ref[g], preferred_element_type=F32))
    r = jax.nn.sigmoid(jnp.concatenate(ga, axis=1) + ba_ref[...])
    gi = jax.nn.sigmoid(jnp.concatenate(gx, axis=1) + bx_ref[...])
    log_a = (-LRU_C) * r * _softplus(-lam_ref[...])
    a = jnp.exp(log_a)
    rows = _row_ids(a.shape, i * tt)
    b = jnp.where(rows >= PAD, jnp.sqrt(-_expm1(2.0 * log_a)) * (gi * xc), 0.0)

    local = lax.broadcasted_iota(jnp.int32, a.shape, 0)
    d = 1
    while d < tt:
        a_sh = jnp.where(local < d, 1.0, pltpu.roll(a, d, 0))
        b_sh = jnp.where(local < d, 0.0, pltpu.roll(b, d, 0))
        b = b + a * b_sh
        a = a * a_sh
        d *= 2
    h = b + a * state_ref[...]
    state_ref[...] = h[tt - 1:tt]
    o_ref[...] = (h * jax.nn.gelu(ly_ref[...])).astype(BF16)


def _lru(u, conv_w, conv_b, wa, ba, wx, bx, lam):
    tp = u.shape[0]
    tt = HEAD
    vec = pl.BlockSpec((1, LRU_WIDTH), lambda i: (0, 0))
    wspec = pl.BlockSpec((LRU_WIDTH // LRU_GROUP, LRU_GROUP, LRU_GROUP), lambda i: (0, 0, 0))
    return pl.pallas_call(
        functools.partial(_lru_kernel, tt=tt),
        grid=(tp // tt,),
        in_specs=[pl.BlockSpec((tt, LRU_WIDTH), lambda i: (i, 0)),
                  pl.BlockSpec((tt, LRU_WIDTH), lambda i: (i, 1)),
                  pl.BlockSpec((8, LRU_WIDTH), lambda i: (0, 0)), vec,
                  wspec, vec, wspec, vec, vec],
        out_specs=pl.BlockSpec((tt, LRU_WIDTH), lambda i: (i, 0)),
        out_shape=jax.ShapeDtypeStruct((tp, LRU_WIDTH), BF16),
        scratch_shapes=[pltpu.VMEM((8, LRU_WIDTH), F32), pltpu.VMEM((1, LRU_WIDTH), F32)],
        compiler_params=_params(("arbitrary",)),
        name="rglru",
    )(u, u, conv_w, conv_b, wa, ba, wx, bx, lam)


def _rwkv_kernel(r_ref, k_ref, vin_ref, lo_ref, mu_ref, mul_ref, wwa_ref, w0_ref, a0_ref, g2_ref,
                 kk_ref, ka_ref, rk_ref, lnw_ref, lnb_ref, seg_ref, o_ref,
                 last_ref, lastl_ref, state_ref, rt_ref, kt_ref, kb_ref, pb_ref, v_ref, gam_ref, y_ref,
                 *, tt):
    i = pl.program_id(0)
    C = RWKV_CHUNK
    G = RWKV_GROUP
    W = RWKV_WIDTH

    @pl.when(i == 0)
    def _():
        last_ref[...] = jnp.zeros_like(last_ref)
        lastl_ref[...] = jnp.zeros_like(lastl_ref)
        state_ref[...] = jnp.zeros_like(state_ref)

    def token_shift(x, mu, last):
        return x + (_shift_rows(x, 1, jnp.tile(last, (8, 1))) - x) * mu

    r_in, k_in, v_in, lo_in = r_ref[...], k_ref[...], vin_ref[...], lo_ref[...]
    r = token_shift(r_in, mu_ref[0:1], last_ref[0:1])
    k = token_shift(k_in, mu_ref[1:2], last_ref[1:2])
    v = token_shift(v_in, mu_ref[2:3], last_ref[2:3])
    lo = token_shift(lo_in, mul_ref[...], lastl_ref[...])
    last_ref[0:1] = r_in[tt - 1:tt]
    last_ref[1:2] = k_in[tt - 1:tt]
    last_ref[2:3] = v_in[tt - 1:tt]
    lastl_ref[...] = lo_in[tt - 1:tt]

    lora = lo[:, 0:RWKV_LORA]
    lane = lax.broadcasted_iota(jnp.int32, lora.shape, 1)
    lora = jnp.where(lane < RWKV_LORA // 2, jnp.tanh(lora), lora)
    wa = _dot(lora, wwa_ref[...])
    w = -_softplus(-(w0_ref[...] + wa[:, 0:W])) - 0.5
    ell = -jnp.exp(w)
    a = jax.nn.sigmoid(a0_ref[...] + wa[:, W:2 * W])
    g = _dot(jax.nn.sigmoid(lo[:, RWKV_LORA:]), g2_ref[...])

    seg = seg_ref[...]
    kk = k * kk_ref[...]
    kk = kk / jnp.maximum(jnp.sqrt(_dot_split(kk * kk, seg)), 1e-12)
    rows = _row_ids(k.shape, i * tt)
    k = jnp.where(rows >= PAD, k * (1.0 + (a - 1.0) * ka_ref[...]), 0.0)
    bonus = _dot_split(r * k * rk_ref[...], seg) * v

    local = lax.broadcasted_iota(jnp.int32, ell.shape, 0) % C
    b = ell
    d = 1
    while d < C:
        b = b + jnp.where(local < d, 0.0, pltpu.roll(b, d, 0))
        d *= 2
    eb = jnp.exp(-b)
    rt_ref[...] = r * jnp.exp(b)
    kt_ref[...] = kk * jnp.exp(b - ell)
    kb_ref[...] = k * eb
    pb_ref[...] = kk * a * eb
    v_ref[...] = v
    gam_ref[...] = jnp.exp(b)

    bd_mask = (lax.broadcasted_iota(jnp.int32, (G, G), 0) // RWKV_HEAD_DIM
               == lax.broadcasted_iota(jnp.int32, (G, G), 1) // RWKV_HEAD_DIM)
    crow = lax.broadcasted_iota(jnp.int32, (C, G), 0)
    ccol = lax.broadcasted_iota(jnp.int32, (C, G), 1) % C
    strict = ccol < crow
    incl = ccol <= crow
    eye = jnp.where(ccol == crow, 1.0, 0.0)

    def bd(x):
        return jnp.where(bd_mask, jnp.tile(x, (G // C, 1)), 0.0)

    def chunk(c, carry):
        rs = pl.ds(pl.multiple_of(c * C, C), C)
        gam = gam_ref[pl.ds(c * C + C - 1, 1), :]
        for gi in range(W // G):
            ls = slice(gi * G, (gi + 1) * G)
            rt = rt_ref[rs, ls]
            kt = kt_ref[rs, ls]
            kb = kb_ref[rs, ls]
            pb = pb_ref[rs, ls]
            vv = v_ref[rs, ls]
            S = state_ref[gi]
            keys = jnp.concatenate([bd(pb), bd(kb)], axis=0)
            ab = _dot_nt(kt, keys)
            q = _dot_nt(rt, keys)
            A = jnp.where(strict, ab[:, 0:G], 0.0)
            B = jnp.where(strict, ab[:, G:2 * G], 0.0)
            Qp = jnp.where(incl, q[:, 0:G], 0.0)
            Qk = jnp.where(incl, q[:, G:2 * G], 0.0)
            X = eye - A
            P = _dot(A, bd(A))
            n = 2
            while n < C:
                X = X + _dot(X, bd(P))
                n *= 2
                if n < C:
                    P = _dot(P, bd(P))
            U = _dot(X, bd(_dot_nt(kt, S) + _dot(B, bd(vv))))
            y_ref[rs, ls] = _dot_nt(rt, S) + _dot(Qk, bd(vv)) - _dot(Qp, bd(U))
            gg = gam[:, ls]
            upd = _dot_tn(jnp.concatenate([vv, U], axis=0),
                          jnp.concatenate([kb * gg, -(pb * gg)], axis=0))
            state_ref[gi] = S * gg + jnp.where(bd_mask, upd, 0.0)
        return carry

    lax.fori_loop(0, tt // C, chunk, 0)

    y = y_ref[...]
    inv_n = 1.0 / RWKV_HEAD_DIM
    mu = _dot_split(y, seg) * inv_n
    dlt = y - mu
    var = _dot_split(dlt * dlt, seg) * inv_n
    yn = dlt * lax.rsqrt(var + RWKV_GN_EPS) * lnw_ref[...] + lnb_ref[...]
    o_ref[...] = ((yn + bonus) * g).astype(BF16)


def _rwkv(u, mu, mu_lora, wwa, w0, a0, g2, k_k, k_a, r_k, ln_w, ln_b, seg):
    tp = u.shape[0]
    tt = HEAD
    W = RWKV_WIDTH
    L2 = 2 * RWKV_LORA
    vec = pl.BlockSpec((1, W), lambda i: (0, 0))
    first = 2 * LRU_WIDTH // W
    big = pltpu.VMEM((tt, W), F32)
    return pl.pallas_call(
        functools.partial(_rwkv_kernel, tt=tt),
        grid=(tp // tt,),
        in_specs=[pl.BlockSpec((tt, W), lambda i: (i, first)),
                  pl.BlockSpec((tt, W), lambda i: (i, first + 1)),
                  pl.BlockSpec((tt, W), lambda i: (i, first + 2)),
                  pl.BlockSpec((tt, L2), lambda i: (i, (first + 3) * W // L2)),
                  pl.BlockSpec((8, W), lambda i: (0, 0)),
                  pl.BlockSpec((1, L2), lambda i: (0, 0)),
                  pl.BlockSpec((RWKV_LORA, 2 * W), lambda i: (0, 0)), vec, vec,
                  pl.BlockSpec((RWKV_LORA, W), lambda i: (0, 0)), vec, vec, vec, vec, vec,
                  pl.BlockSpec((W, W), lambda i: (0, 0))],
        out_specs=pl.BlockSpec((tt, W), lambda i: (i, 0)),
        out_shape=jax.ShapeDtypeStruct((tp, W), BF16),
        scratch_shapes=[pltpu.VMEM((8, W), F32), pltpu.VMEM((1, L2), F32),
                        pltpu.VMEM((W // RWKV_GROUP, RWKV_GROUP, RWKV_GROUP), F32),
                        big, big, big, big, big, big, big],
        compiler_params=_params(("arbitrary",)),
        name="rwkv7",
    )(u, u, u, u, mu, mu_lora, wwa, w0, a0, g2, k_k, k_a, r_k, ln_w, ln_b, seg)


def _gla_kernel(q_ref, k_ref, v_ref, g_ref, gd_ref, up_ref, gb_ref, nw_ref, tril_ref, ones_ref,
                o_ref, state_ref, gk_ref, *, tt):
    i = pl.program_id(0)
    C = GLA_CHUNK
    SB = GLA_SUB
    DK = GLA_DK
    DV = GLA_DV
    NEG = -jnp.inf

    @pl.when(i == 0)
    def _():
        state_ref[...] = jnp.zeros_like(state_ref)

    z = _dot(gd_ref[...], up_ref[...]) + gb_ref[...]
    gk_ref[...] = -_softplus(-z) * (1.0 / GLA_GATE_NORM)

    crow = lax.broadcasted_iota(jnp.int32, (C, DK), 0)
    sub_row = crow % SB
    lane128 = lax.broadcasted_iota(jnp.int32, (C, 128), 1)
    row128 = lax.broadcasted_iota(jnp.int32, (C, 128), 0)
    scale = DK ** -0.5

    def chunk(c, carry):
        rs = pl.ds(pl.multiple_of(c * C, C), C)
        valid = (crow + (i * tt + c * C)) >= PAD
        valid_v = (lax.broadcasted_iota(jnp.int32, (C, DV), 0) + (i * tt + c * C)) >= PAD
        for hd in range(GLA_HEADS):
            ks = slice(hd * DK, (hd + 1) * DK)
            vs = slice(hd * DV, (hd + 1) * DV)
            gk = gk_ref[rs, ks]
            b = _cumsum_rows(gk, tril_ref[...])
            q = q_ref[rs, ks] * scale
            k = jnp.where(valid, k_ref[rs, ks], 0.0)
            v = jnp.where(valid_v, v_ref[rs, vs], 0.0)
            S = state_ref[hd]
            b_last = b[C - 1:C]
            o = _dot_nt(q * jnp.exp(b), S)

            blocks = [jnp.zeros((SB, C), F32)]
            for sb in range(1, C // SB):
                ref_b = b[sb * SB - 1:sb * SB]
                qh = q[sb * SB:(sb + 1) * SB] * jnp.exp(b[sb * SB:(sb + 1) * SB] - ref_b)
                kh = k * jnp.exp(jnp.where(crow < sb * SB, ref_b - b, NEG))
                blocks.append(_dot_nt(qh, kh))
            A = jnp.concatenate(blocks, axis=0)

            diag = jnp.zeros((C, 128), F32)
            for jj in range(SB):
                ksel = jnp.concatenate(
                    [jnp.broadcast_to(k[s * SB + jj:s * SB + jj + 1], (SB, DK)) for s in range(C // SB)], axis=0)
                bsel = jnp.concatenate(
                    [jnp.broadcast_to(b[s * SB + jj:s * SB + jj + 1], (SB, DK)) for s in range(C // SB)], axis=0)
                e = jnp.exp(jnp.where(sub_row >= jj, b - bsel, NEG))
                col = _dot(q * e * ksel, ones_ref[...])
                diag = jnp.where(lane128 % SB == jj, col, diag)
            diag = jnp.where((lane128 // SB == row128 // SB) & (lane128 < C), diag, 0.0)
            A = A + diag[:, 0:C]

            o = o + _dot(A, v)
            on = o * lax.rsqrt(jnp.mean(o * o, axis=-1, keepdims=True) + NORM_EPS) * nw_ref[...]
            gate = g_ref[rs, vs]
            o_ref[rs, vs] = (on * (gate * jax.nn.sigmoid(gate))).astype(BF16)
            state_ref[hd] = S * jnp.exp(b_last) + _dot_tn(v, k * jnp.exp(b_last - b))
        return carry

    lax.fori_loop(0, tt // C, chunk, 0)


def _cumsum_rows(x, tril_bf16):
    hi = x.astype(BF16)
    lo = (x - hi.astype(F32)).astype(BF16)
    return (jnp.dot(tril_bf16, hi, preferred_element_type=F32)
            + jnp.dot(tril_bf16, lo, preferred_element_type=F32))


def _gla(ug, gk_up, gk_b, norm_w, tril, ones):
    tp = ug.shape[0]
    tt = HEAD
    kb = GLA_KW
    return pl.pallas_call(
        functools.partial(_gla_kernel, tt=tt),
        grid=(tp // tt,),
        in_specs=[pl.BlockSpec((tt, GLA_KW), lambda i: (i, 0)),
                  pl.BlockSpec((tt, GLA_KW), lambda i: (i, 1)),
                  pl.BlockSpec((tt, GLA_VW), lambda i: (i, 1)),
                  pl.BlockSpec((tt, GLA_VW), lambda i: (i, 2)),
                  pl.BlockSpec((tt, 128), lambda i: (i, GLA_MAIN // 128)),
                  pl.BlockSpec((128, GLA_KW), lambda i: (0, 0)),
                  pl.BlockSpec((1, GLA_KW), lambda i: (0, 0)),
                  pl.BlockSpec((1, GLA_DV), lambda i: (0, 0)),
                  pl.BlockSpec((GLA_CHUNK, GLA_CHUNK), lambda i: (0, 0)),
                  pl.BlockSpec((GLA_DK, 128), lambda i: (0, 0))],
        out_specs=pl.BlockSpec((tt, GLA_VW), lambda i: (i, 0)),
        out_shape=jax.ShapeDtypeStruct((tp, GLA_VW), BF16),
        scratch_shapes=[pltpu.VMEM((GLA_HEADS, GLA_DV, GLA_DK), F32),
                        pltpu.VMEM((tt, GLA_KW), F32)],
        compiler_params=_params(("arbitrary",)),
        name="gla",
    )(ug, ug, ug, ug, ug, gk_up, gk_b, norm_w, tril, ones)


def _block_diag_groups(w):
    nb = LRU_GROUP // LRU_BLOCK
    w4 = w.reshape(-1, nb, LRU_BLOCK, LRU_BLOCK)
    out = jnp.einsum("gaij,ab->gaibj", w4, jnp.eye(nb, dtype=w.dtype))
    return out.reshape(-1, LRU_GROUP, LRU_GROUP)


def _row(v):
    return v.reshape(1, -1).astype(F32)


def _ffn(h, hn, layer, w_gate, w_val, conv_w, conv_b, w_down, gpost, gnext):
    cw = jnp.zeros((8, D_FF), F32).at[:FFN_CONV].set(conv_w[layer])
    act = _ffn_up(hn, w_gate[layer].astype(BF16), w_val[layer].astype(BF16), cw, _row(conv_b[layer]))
    return _mm_res([act], [w_down[layer].astype(BF16)], h, gpost, gnext, 1408, f"ffn_down{layer}")


def kernel(x, meta_tokens, mix_pre_norm, mix_post_norm, ffn_pre_norm, ffn_post_norm, ab_w_in, lru_conv_w, lru_conv_b, lru_gate_a_w, lru_gate_a_b, lru_gate_x_w, lru_gate_x_b, lru_lambda, rwkv_shift_mu, rwkv_w0, rwkv_w2, rwkv_a0, rwkv_a2, rwkv_g2, rwkv_k_k, rwkv_k_a, rwkv_r_k, rwkv_ln_w, rwkv_ln_b, ab_w_out, gla_w_in, gla_gk_up, gla_gk_b, gla_norm_w, gla_w_out, ffn_w_gate, ffn_w_val, ffn_conv_w, ffn_conv_b, ffn_w_down):
    batch, seq, _ = x.shape
    assert batch == 1 and seq % HEAD == 0
    W = RWKV_WIDTH
    head = jnp.concatenate([jnp.zeros((PAD, D_MODEL), F32), meta_tokens.astype(F32)], axis=0)
    h, hn = _embed(head, x[0], _row(mix_pre_norm[0]))

    u = _mm(hn, ab_w_in[0].astype(BF16), 768, "ab_in")
    cw = jnp.zeros((8, LRU_WIDTH), F32).at[:LRU_CONV].set(lru_conv_w[0])
    lru_out = _lru(u, cw, _row(lru_conv_b[0]),
                   _block_diag_groups(lru_gate_a_w[0]).astype(BF16), _row(lru_gate_a_b[0]),
                   _block_diag_groups(lru_gate_x_w[0]).astype(BF16), _row(lru_gate_x_b[0]),
                   _row(lru_lambda[0]))
    half = RWKV_LORA // 2
    wwa = jnp.zeros((RWKV_LORA, 2 * W), F32)
    wwa = wwa.at[:half, :W].set(rwkv_w2[0]).at[half:, W:].set(rwkv_a2[0]).astype(BF16)
    seg = jnp.kron(jnp.eye(W // RWKV_HEAD_DIM, dtype=F32),
                   jnp.ones((RWKV_HEAD_DIM, RWKV_HEAD_DIM), F32)).astype(BF16)
    mu = jnp.zeros((8, W), F32).at[:3].set(rwkv_shift_mu[0, :3 * W].reshape(3, W))
    rw_out = _rwkv(u, mu, _row(rwkv_shift_mu[0, 3 * W:]), wwa, _row(rwkv_w0[0]), _row(rwkv_a0[0]),
                   rwkv_g2[0].astype(BF16), _row(rwkv_k_k[0]), _row(rwkv_k_a[0]), _row(rwkv_r_k[0]),
                   _row(rwkv_ln_w[0]), _row(rwkv_ln_b[0]), seg)
    w_out = ab_w_out[0].astype(BF16)
    h, hn = _mm_res([lru_out, rw_out], [w_out[:LRU_WIDTH], w_out[LRU_WIDTH:]], h,
                    _row(mix_post_norm[0]), _row(ffn_pre_norm[0]), 1024, "ab_out")
    h, hn = _ffn(h, hn, 0, ffn_w_gate, ffn_w_val, ffn_conv_w, ffn_conv_b, ffn_w_down,
                 _row(ffn_post_norm[0]), _row(mix_pre_norm[1]))

    w_in = jnp.zeros((D_MODEL, GLA_IN_PAD), F32).at[:, :gla_w_in.shape[2]].set(gla_w_in[0]).astype(BF16)
    ug = _mm(hn, w_in, 896, "gla_in")
    up = jnp.zeros((128, GLA_KW), F32).at[:GLA_GATE_RANK].set(gla_gk_up[0]).astype(BF16)
    tril = jnp.tril(jnp.ones((GLA_CHUNK, GLA_CHUNK), F32)).astype(BF16)
    ones = jnp.ones((GLA_DK, 128), BF16)
    o = _gla(ug, up, _row(gla_gk_b[0]), _row(gla_norm_w[0]), tril, ones)
    h, hn = _mm_res([o], [gla_w_out[0].astype(BF16)], h,
                    _row(mix_post_norm[1]), _row(ffn_pre_norm[1]), 1024, "gla_out")
    h, _ = _ffn(h, hn, 1, ffn_w_gate, ffn_w_val, ffn_conv_w, ffn_conv_b, ffn_w_down,
                _row(ffn_post_norm[1]), None)
    return h[HEAD:][None]
```

```python
import functools
import math

import jax
import jax.numpy as jnp
from jax import lax
from jax.experimental import pallas as pl
from jax.experimental.pallas import tpu as pltpu

F32 = jnp.float32
BF16 = jnp.bfloat16

D_MODEL = 2048
N_META = 16
NORM_EPS = 1e-6
HEAD = 256
PAD = HEAD - N_META
LRU_WIDTH = 1024
LRU_BLOCK = 64
LRU_GROUP = 256
LRU_CONV = 4
LRU_C = 8.0
RWKV_WIDTH = 1024
RWKV_HEAD_DIM = 64
RWKV_LORA = 128
RWKV_GN_EPS = 64e-5
RWKV_CHUNK = 64
RWKV_GROUP = 128
GLA_HEADS = 4
GLA_DK = 256
GLA_DV = 512
GLA_KW = GLA_HEADS * GLA_DK
GLA_VW = GLA_HEADS * GLA_DV
GLA_GATE_RANK = 16
GLA_GATE_NORM = 16.0
GLA_CHUNK = 64
GLA_SUB = 8
GLA_MAIN = 2 * GLA_KW + 2 * GLA_VW
D_FF = 5632
FFN_CONV = 3
LOG2E = 1.4426950408889634
VMEM_LIMIT = 56 * 1024 * 1024


def _pick(n, candidates):
    for c in candidates:
        if n % c == 0:
            return c
    raise ValueError(f"no tile for {n} in {candidates}")


def _params(sem):
    return pltpu.CompilerParams(dimension_semantics=sem, vmem_limit_bytes=VMEM_LIMIT)


def _dot(a, b):
    return jnp.dot(a.astype(BF16), b.astype(BF16), preferred_element_type=F32)


def _dot_nt(a, b):
    return lax.dot_general(a.astype(BF16), b.astype(BF16), (((1,), (1,)), ((), ())),
                           preferred_element_type=F32)


def _dot_tn(a, b):
    return lax.dot_general(a.astype(BF16), b.astype(BF16), (((0,), (0,)), ((), ())),
                           preferred_element_type=F32)


def _dot_split(a, b_bf16):
    hi = a.astype(BF16)
    lo = (a - hi.astype(F32)).astype(BF16)
    return (jnp.dot(hi, b_bf16, preferred_element_type=F32)
            + jnp.dot(lo, b_bf16, preferred_element_type=F32))


def _softplus(x):
    return jnp.maximum(x, 0.0) + jnp.log1p(jnp.exp(-jnp.abs(x)))


def _expm1(x):
    u = jnp.exp(x)
    d = u - 1.0
    return jnp.where(d == 0.0, x, d * x / jnp.log(jnp.where(d == 0.0, 2.0, u)))


def _rms(x):
    return x * lax.rsqrt(jnp.mean(x * x, axis=-1, keepdims=True) + NORM_EPS)


def _row_ids(shape, base):
    return lax.broadcasted_iota(jnp.int32, shape, 0) + base


def _shift_rows(x, s, prev8):
    rolled = pltpu.roll(x, s, 0)
    rows = lax.broadcasted_iota(jnp.int32, prev8.shape, 0)
    top = jnp.where(rows < s, pltpu.roll(prev8, s, 0), rolled[0:8])
    return jnp.concatenate([top, rolled[8:]], axis=0)


def _time_block(nb):
    return lambda i: ((i + nb - 1) % nb)


def _embed_kernel(head_ref, x_ref, g_ref, h_ref, hn_ref):
    i = pl.program_id(0)
    xb = jnp.where(i == pl.num_programs(0) - 1, head_ref[...], x_ref[...])
    h_ref[...] = xb
    hn_ref[...] = (_rms(xb) * g_ref[...]).astype(BF16)


def _embed(head, x2d, gain):
    seq = x2d.shape[0]
    tp = seq + HEAD
    nb = tp // HEAD
    blk = pl.BlockSpec((HEAD, D_MODEL), lambda i: (i, 0))
    return pl.pallas_call(
        _embed_kernel,
        grid=(nb,),
        in_specs=[pl.BlockSpec((HEAD, D_MODEL), lambda i: (0, 0)),
                  pl.BlockSpec((HEAD, D_MODEL), lambda i: (jnp.minimum(i, nb - 2), 0)),
                  pl.BlockSpec((1, D_MODEL), lambda i: (0, 0))],
        out_specs=[blk, blk],
        out_shape=[jax.ShapeDtypeStruct((tp, D_MODEL), F32),
                   jax.ShapeDtypeStruct((tp, D_MODEL), BF16)],
        compiler_params=_params(("arbitrary",)),
        name="embed",
    )(head, x2d, gain)


def _mm_kernel(x_ref, w_ref, o_ref):
    o_ref[...] = jnp.dot(x_ref[...], w_ref[...].astype(BF16), preferred_element_type=F32)


def _mm(x, w3, n, tn, name):
    tp, k = x.shape
    tm = _pick(tp, (1280, 768, 256))
    return pl.pallas_call(
        _mm_kernel,
        grid=(tp // tm, n // tn),
        in_specs=[pl.BlockSpec((tm, k), lambda i, j: (i, 0)),
                  pl.BlockSpec((None, k, tn), lambda i, j: (0, 0, j))],
        out_specs=pl.BlockSpec((tm, tn), lambda i, j: (i, j)),
        out_shape=jax.ShapeDtypeStruct((tp, n), F32),
        compiler_params=_params(("arbitrary", "arbitrary")),
        name=name,
    )(x, w3)


def _mm_res_kernel(*refs, n_x, tm, tn, seq, has_next):
    x_refs = refs[:n_x]
    w_refs = refs[n_x:2 * n_x]
    h_ref, gpost_ref = refs[2 * n_x:2 * n_x + 2]
    rest = refs[2 * n_x + 2:]
    if has_next:
        gnext_ref, ho_ref, hn_ref, acc_ref = rest
    else:
        ho_ref, acc_ref = rest
    i = pl.program_id(0)
    j = pl.program_id(1)
    part = jnp.dot(x_refs[0][...], w_refs[0][...], preferred_element_type=F32)
    for xr, wr in zip(x_refs[1:], w_refs[1:]):
        part = part + jnp.dot(xr[...], wr[...], preferred_element_type=F32)
    acc_ref[j] = part

    @pl.when(j == pl.num_programs(1) - 1)
    def _():
        nt = D_MODEL // tn
        ssq = jnp.zeros((tm, 1), F32)
        for t in range(nt):
            m = acc_ref[t]
            ssq = ssq + jnp.sum(m * m, axis=-1, keepdims=True)
        scale = lax.rsqrt(ssq * (1.0 / D_MODEL) + NORM_EPS)
        rows = _row_ids((tm, 1), i * tm)
        scale = jnp.where((rows < seq) | (rows >= seq + PAD), scale, 0.0)
        ssq2 = jnp.zeros((tm, 1), F32)
        for t in range(nt):
            cs = slice(t * tn, (t + 1) * tn)
            hnew = h_ref[:, cs] + acc_ref[t] * scale * gpost_ref[:, cs]
            ho_ref[:, cs] = hnew
            if has_next:
                ssq2 = ssq2 + jnp.sum(hnew * hnew, axis=-1, keepdims=True)
        if has_next:
            scale2 = lax.rsqrt(ssq2 * (1.0 / D_MODEL) + NORM_EPS)
            for t in range(nt):
                cs = slice(t * tn, (t + 1) * tn)
                hn_ref[:, cs] = (ho_ref[:, cs] * scale2 * gnext_ref[:, cs]).astype(BF16)


def _mm_res(xs, ws, h, gpost, gnext, tm_candidates, name):
    tp = h.shape[0]
    seq = tp - HEAD
    kdim = xs[0].shape[1]
    has_next = gnext is not None
    rows_out = tp if has_next else seq
    tm = _pick(rows_out, tm_candidates)
    tn = 512
    row = pl.BlockSpec((tm, D_MODEL), lambda i, j: (i, 0))
    gain = pl.BlockSpec((1, D_MODEL), lambda i, j: (0, 0))
    in_specs = ([pl.BlockSpec((tm, kdim), lambda i, j: (i, 0)) for _ in xs]
                + [pl.BlockSpec((kdim, tn), lambda i, j: (0, j)) for _ in ws]
                + [row, gain] + ([gain] if has_next else []))
    out_specs = [row] + ([row] if has_next else [])
    out_shape = [jax.ShapeDtypeStruct((rows_out, D_MODEL), F32)]
    if has_next:
        out_shape.append(jax.ShapeDtypeStruct((tp, D_MODEL), BF16))
    args = list(xs) + list(ws) + [h, gpost] + ([gnext] if has_next else [])
    out = pl.pallas_call(
        functools.partial(_mm_res_kernel, n_x=len(xs), tm=tm, tn=tn, seq=seq, has_next=has_next),
        grid=(rows_out // tm, D_MODEL // tn),
        in_specs=in_specs,
        out_specs=out_specs,
        out_shape=out_shape,
        scratch_shapes=[pltpu.VMEM((D_MODEL // tn, tm, tn), F32)],
        compiler_params=_params(("arbitrary", "arbitrary")),
        name=name,
    )(*args)
    return out if has_next else (out[0], None)


def _ffn_up_kernel(x_ref, xh_ref, wg_ref, wv_ref, cw_ref, cb_ref, o_ref):
    x = x_ref[...]
    wg = wg_ref[...].astype(BF16)
    gate = jnp.dot(x, wg, preferred_element_type=F32)
    val = jnp.dot(x, wv_ref[...].astype(BF16), preferred_element_type=F32)
    halo = jnp.dot(xh_ref[...], wg, preferred_element_type=F32)[8:16]
    cw = cw_ref[...]
    conv = (cw[2:3] * gate + cw[1:2] * _shift_rows(gate, 1, halo)
            + cw[0:1] * _shift_rows(gate, 2, halo) + cb_ref[...])
    o_ref[...] = (jax.nn.gelu(conv) * val).astype(BF16)


def _ffn_up(hn, w_gate, w_val, layer, conv_w, conv_b):
    tp = hn.shape[0]
    tm = _pick(tp, (1280, 768, 256))
    tn = 512
    halo_blocks = tm // 16
    nhalo = tp // 16
    wspec = pl.BlockSpec((None, D_MODEL, tn), lambda i, j: (layer, 0, j))
    return pl.pallas_call(
        _ffn_up_kernel,
        grid=(tp // tm, D_FF // tn),
        in_specs=[pl.BlockSpec((tm, D_MODEL), lambda i, j: (i, 0)),
                  pl.BlockSpec((16, D_MODEL), lambda i, j: ((i * halo_blocks + nhalo - 1) % nhalo, 0)),
                  wspec, wspec,
                  pl.BlockSpec((8, tn), lambda i, j: (0, j)),
                  pl.BlockSpec((1, tn), lambda i, j: (0, j))],
        out_specs=pl.BlockSpec((tm, tn), lambda i, j: (i, j)),
        out_shape=jax.ShapeDtypeStruct((tp, D_FF), BF16),
        compiler_params=_params(("arbitrary", "arbitrary")),
        name="ffn_up",
    )(hn, hn, w_gate, w_val, conv_w, conv_b)


def _lru_kernel(lx_ref, ly_ref, cw_ref, cb_ref, wa_ref, ba_ref, wx_ref, bx_ref, lam_ref,
                o_ref, halo_ref, state_ref, *, tt):
    i = pl.program_id(0)

    @pl.when(i == 0)
    def _():
        halo_ref[...] = jnp.zeros_like(halo_ref)
        state_ref[...] = jnp.zeros_like(state_ref)

    lx = lx_ref[...]
    prev = halo_ref[...]
    cw = cw_ref[...]
    xc = (cw[3:4] * lx + cw[2:3] * _shift_rows(lx, 1, prev) + cw[1:2] * _shift_rows(lx, 2, prev)
          + cw[0:1] * _shift_rows(lx, 3, prev) + cb_ref[...])
    halo_ref[...] = lx[tt - 8:tt]

    xb = xc.astype(BF16)
    ga, gx = [], []
    for g in range(LRU_WIDTH // LRU_GROUP):
        blk = xb[:, g * LRU_GROUP:(g + 1) * LRU_GROUP]
        ga.append(jnp.dot(blk, wa_ref[g], preferred_element_type=F32))
        gx.append(jnp.dot(blk, wx_ref[g], preferred_element_type=F32))
    r = jax.nn.sigmoid(jnp.concatenate(ga, axis=1) + ba_ref[...])
    gi = jax.nn.sigmoid(jnp.concatenate(gx, axis=1) + bx_ref[...])
    log_a = (-LRU_C) * r * _softplus(-lam_ref[...])
    a = jnp.exp(log_a)
    rows = _row_ids(a.shape, i * tt)
    b = jnp.where(rows >= PAD, jnp.sqrt(-_expm1(2.0 * log_a)) * (gi * xc), 0.0)

    local = lax.broadcasted_iota(jnp.int32, a.shape, 0)
    d = 1
    while d < tt:
        if d < 8:
            a_sh = jnp.where(local < d, 1.0, pltpu.roll(a, d, 0))
            b_sh = jnp.where(local < d, 0.0, pltpu.roll(b, d, 0))
        else:
            a_sh = jnp.concatenate([jnp.ones((d, LRU_WIDTH), F32), a[0:tt - d]], axis=0)
            b_sh = jnp.concatenate([jnp.zeros((d, LRU_WIDTH), F32), b[0:tt - d]], axis=0)
        b = b + a * b_sh
        a = a * a_sh
        d *= 2
    h = b + a * state_ref[...]
    state_ref[...] = h[tt - 1:tt]
    o_ref[...] = (h * jax.nn.gelu(ly_ref[...])).astype(BF16)


def _lru(u, conv_w, conv_b, wa, ba, wx, bx, lam):
    tp = u.shape[0]
    tt = HEAD
    tb = _time_block(tp // tt)
    vec = pl.BlockSpec((1, LRU_WIDTH), lambda i: (0, 0))
    wspec = pl.BlockSpec((LRU_WIDTH // LRU_GROUP, LRU_GROUP, LRU_GROUP), lambda i: (0, 0, 0))
    return pl.pallas_call(
        functools.partial(_lru_kernel, tt=tt),
        grid=(tp // tt,),
        in_specs=[pl.BlockSpec((tt, LRU_WIDTH), lambda i: (tb(i), 0)),
                  pl.BlockSpec((tt, LRU_WIDTH), lambda i: (tb(i), 1)),
                  pl.BlockSpec((8, LRU_WIDTH), lambda i: (0, 0)), vec,
                  wspec, vec, wspec, vec, vec],
        out_specs=pl.BlockSpec((tt, LRU_WIDTH), lambda i: (tb(i), 0)),
        out_shape=jax.ShapeDtypeStruct((tp, LRU_WIDTH), BF16),
        scratch_shapes=[pltpu.VMEM((8, LRU_WIDTH), F32), pltpu.VMEM((1, LRU_WIDTH), F32)],
        compiler_params=_params(("arbitrary",)),
        name="rglru",
    )(u, u, conv_w, conv_b, wa, ba, wx, bx, lam)


def _rwkv_kernel(r_ref, k_ref, vin_ref, lo_ref, mu_ref, mul_ref, wwa_ref, w0_ref, a0_ref, g2_ref,
                 kk_ref, ka_ref, rk_ref, lnw_ref, lnb_ref, seg_ref, tril_ref, o_ref,
                 last_ref, lastl_ref, state_ref, rt_ref, kt_ref, kb_ref, pb_ref, v_ref, gam_ref, y_ref,
                 x_ref, bv_ref, qkv_ref, qp_ref, *, tt):
    i = pl.program_id(0)
    C = RWKV_CHUNK
    G = RWKV_GROUP
    W = RWKV_WIDTH
    NG = W // G

    @pl.when(i == 0)
    def _():
        last_ref[...] = jnp.zeros_like(last_ref)
        lastl_ref[...] = jnp.zeros_like(lastl_ref)
        state_ref[...] = jnp.zeros_like(state_ref)

    def token_shift(x, mu, last):
        return x + (_shift_rows(x, 1, jnp.tile(last, (8, 1))) - x) * mu

    r_in, k_in, v_in, lo_in = r_ref[...], k_ref[...], vin_ref[...], lo_ref[...]
    r = token_shift(r_in, mu_ref[0:1], last_ref[0:1])
    k = token_shift(k_in, mu_ref[1:2], last_ref[1:2])
    v = token_shift(v_in, mu_ref[2:3], last_ref[2:3])
    lo = token_shift(lo_in, mul_ref[...], lastl_ref[...])
    last_ref[0:1] = r_in[tt - 1:tt]
    last_ref[1:2] = k_in[tt - 1:tt]
    last_ref[2:3] = v_in[tt - 1:tt]
    lastl_ref[...] = lo_in[tt - 1:tt]

    lora = lo[:, 0:RWKV_LORA]
    lane = lax.broadcasted_iota(jnp.int32, lora.shape, 1)
    lora = jnp.where(lane < RWKV_LORA // 2, jnp.tanh(lora), lora)
    wa = _dot(lora, wwa_ref[...])
    ell = (-math.exp(-0.5) * LOG2E) * jax.nn.sigmoid(w0_ref[...] + wa[:, 0:W])
    a = jax.nn.sigmoid(a0_ref[...] + wa[:, W:2 * W])
    g = _dot(jax.nn.sigmoid(lo[:, RWKV_LORA:]), g2_ref[...])

    seg = seg_ref[...]

    def head_sums(x):
        return jnp.concatenate([_dot(x[:, t * G:(t + 1) * G], seg) for t in range(NG)], axis=1)

    kk = k * kk_ref[...]
    kk = kk / jnp.maximum(jnp.sqrt(head_sums(kk * kk)), 1e-12)
    rows = _row_ids(k.shape, i * tt)
    k = jnp.where(rows >= PAD, k * (1.0 + (a - 1.0) * ka_ref[...]), 0.0)
    bonus = head_sums(r * k * rk_ref[...]) * v

    b = _cumsum_rows(ell, tril_ref[...])
    eb = jnp.exp2(-b)
    gam = jnp.exp2(b)
    rt_ref[...] = r * gam
    kt_ref[...] = kk * jnp.exp2(b - ell)
    kb_ref[...] = k * eb
    pb_ref[...] = kk * a * eb
    v_ref[...] = v
    gam_ref[...] = gam

    crow = lax.broadcasted_iota(jnp.int32, (C, G), 0)
    ccol = lax.broadcasted_iota(jnp.int32, (C, G), 1) % C
    strict = ccol < crow
    incl = ccol <= crow
    eye = jnp.where(ccol == crow, 1.0, 0.0)
    groups = range(NG)

    def bd(x):
        xb = x.astype(BF16)
        return jnp.concatenate([xb, xb], axis=0) * seg

    chunks = range(tt // C)

    blocks = [(slice(c * C, (c + 1) * C), slice(gi * G, (gi + 1) * G)) for c in chunks for gi in groups]

    keys = [jnp.concatenate([bd(pb_ref[rs, ls]), bd(kb_ref[rs, ls])], axis=0) for rs, ls in blocks]
    ab = [_dot_nt(kt_ref[rs, ls], keys[n]) for n, (rs, ls) in enumerate(blocks)]
    for n, (rs, ls) in enumerate(blocks):
        q = _dot_nt(rt_ref[rs, ls], keys[n])
        bdv = bd(v_ref[rs, ls])
        bv_ref[rs, ls] = _dot(jnp.where(strict, ab[n][:, G:2 * G], 0.0), bdv)
        qkv_ref[rs, ls] = _dot(jnp.where(incl, q[:, G:2 * G], 0.0), bdv)
        qp_ref[rs, ls] = jnp.where(incl, q[:, 0:G], 0.0)
    A = [jnp.where(strict, m[:, 0:G], 0.0) for m in ab]
    X = [eye - m for m in A]
    P = [_dot(m, bd(m)) for m in A]
    n_pow = 2
    while n_pow < C:
        X = [x + _dot(x, bd(p)) for x, p in zip(X, P)]
        n_pow *= 2
        if n_pow < C:
            P = [_dot(p, bd(p)) for p in P]
    for n, (rs, ls) in enumerate(blocks):
        x_ref[rs, ls] = X[n]

    S = [state_ref[gi] for gi in groups]
    for c in chunks:
        rs = slice(c * C, (c + 1) * C)
        cols = [slice(gi * G, (gi + 1) * G) for gi in groups]
        kts = [_dot_nt(kt_ref[rs, ls], S[gi]) for gi, ls in enumerate(cols)]
        rts = [_dot_nt(rt_ref[rs, ls], S[gi]) for gi, ls in enumerate(cols)]
        U = [_dot(x_ref[rs, ls], bd(kts[gi] + bv_ref[rs, ls])) for gi, ls in enumerate(cols)]
        gg = [gam_ref[(c + 1) * C - 1:(c + 1) * C, ls] for ls in cols]
        upd = [_dot_tn(jnp.concatenate([v_ref[rs, ls], U[gi]], axis=0),
                       jnp.concatenate([kb_ref[rs, ls] * gg[gi], -(pb_ref[rs, ls] * gg[gi])], axis=0))
               for gi, ls in enumerate(cols)]
        for gi, ls in enumerate(cols):
            y_ref[rs, ls] = rts[gi] + qkv_ref[rs, ls] - _dot(qp_ref[rs, ls], bd(U[gi]))
        S = [S[gi] * gg[gi] + upd[gi] * seg for gi in groups]
    for gi in groups:
        state_ref[gi] = S[gi]

    y = y_ref[...]
    inv_n = 1.0 / RWKV_HEAD_DIM
    mu = head_sums(y) * inv_n
    dlt = y - mu
    var = head_sums(dlt * dlt) * inv_n
    yn = dlt * lax.rsqrt(var + RWKV_GN_EPS) * lnw_ref[...] + lnb_ref[...]
    o_ref[...] = ((yn + bonus) * g).astype(BF16)


def _rwkv(u, mu, mu_lora, wwa, w0, a0, g2, k_k, k_a, r_k, ln_w, ln_b, seg, tril):
    tp = u.shape[0]
    tt = HEAD
    tb = _time_block(tp // tt)
    W = RWKV_WIDTH
    G = RWKV_GROUP
    L2 = 2 * RWKV_LORA
    vec = pl.BlockSpec((1, W), lambda i: (0, 0))
    first = 2 * LRU_WIDTH // W
    big = pltpu.VMEM((tt, W), F32)
    return pl.pallas_call(
        functools.partial(_rwkv_kernel, tt=tt),
        grid=(tp // tt,),
        in_specs=[pl.BlockSpec((tt, W), lambda i: (tb(i), first)),
                  pl.BlockSpec((tt, W), lambda i: (tb(i), first + 1)),
                  pl.BlockSpec((tt, W), lambda i: (tb(i), first + 2)),
                  pl.BlockSpec((tt, L2), lambda i: (tb(i), (first + 3) * W // L2)),
                  pl.BlockSpec((8, W), lambda i: (0, 0)),
                  pl.BlockSpec((1, L2), lambda i: (0, 0)),
                  pl.BlockSpec((RWKV_LORA, 2 * W), lambda i: (0, 0)), vec, vec,
                  pl.BlockSpec((RWKV_LORA, W), lambda i: (0, 0)), vec, vec, vec, vec, vec,
                  pl.BlockSpec((G, G), lambda i: (0, 0)),
                  pl.BlockSpec((tt, tt), lambda i: (0, 0))],
        out_specs=pl.BlockSpec((tt, W), lambda i: (tb(i), 0)),
        out_shape=jax.ShapeDtypeStruct((tp, W), BF16),
        scratch_shapes=[pltpu.VMEM((8, W), F32), pltpu.VMEM((1, L2), F32),
                        pltpu.VMEM((W // G, G, G), F32)] + [big] * 11,
        compiler_params=_params(("arbitrary",)),
        name="rwkv7",
    )(u, u, u, u, mu, mu_lora, wwa, w0, a0, g2, k_k, k_a, r_k, ln_w, ln_b, seg, tril)


def _cumsum_rows(x, tril_bf16):
    hi = x.astype(BF16)
    lo = (x - hi.astype(F32)).astype(BF16)
    return (jnp.dot(tril_bf16, hi, preferred_element_type=F32)
            + jnp.dot(tril_bf16, lo, preferred_element_type=F32))


def _gla_kernel(q_ref, k_ref, v_ref, g_ref, hn_ref, wgd_ref, up_ref, gb_ref, nw_ref, tril_ref, ones_ref,
                o_ref, state_ref, gk_ref, *, tt):
    i = pl.program_id(0)
    C = GLA_CHUNK
    SB = GLA_SUB
    DK = GLA_DK
    DV = GLA_DV
    NEG = -jnp.inf

    @pl.when(i == 0)
    def _():
        state_ref[...] = jnp.zeros_like(state_ref)

    gd = jnp.dot(hn_ref[...], wgd_ref[...], preferred_element_type=F32)
    z = _dot(gd, up_ref[...]) + gb_ref[...]
    gk_ref[...] = -_softplus(-z) * (LOG2E / GLA_GATE_NORM)

    crow = lax.broadcasted_iota(jnp.int32, (C, DK), 0)
    sub_row = crow % SB
    lane128 = lax.broadcasted_iota(jnp.int32, (C, 128), 1)
    row128 = lax.broadcasted_iota(jnp.int32, (C, 128), 0)
    diag_mask = (lane128 // SB == row128 // SB) & (lane128 < C)
    scale = DK ** -0.5

    def chunk(c, carry):
        rs = pl.ds(pl.multiple_of(c * C, C), C)
        base = i * tt + c * C
        valid = (crow + base) >= PAD
        valid_v = (lax.broadcasted_iota(jnp.int32, (C, DV), 0) + base) >= PAD
        for hd in range(GLA_HEADS):
            ks = slice(hd * DK, (hd + 1) * DK)
            vs = slice(hd * DV, (hd + 1) * DV)
            b = _cumsum_rows(gk_ref[rs, ks], tril_ref[...])
            q = q_ref[rs, ks] * scale
            k = jnp.where(valid, k_ref[rs, ks], 0.0)
            v = jnp.where(valid_v, v_ref[rs, vs], 0.0)
            S = state_ref[hd]
            b_last = b[C - 1:C]
            o = _dot_nt(q * jnp.exp2(b), S)

            blocks = [jnp.zeros((SB, C), F32)]
            for sb in range(1, C // SB):
                r0 = sb * SB
                ref_b = b[r0 - 1:r0]
                qh = q[r0:r0 + SB] * jnp.exp2(b[r0:r0 + SB] - ref_b)
                kh = k[0:r0] * jnp.exp2(ref_b - b[0:r0])
                kh = jnp.concatenate([kh, jnp.zeros((C - r0, DK), F32)], axis=0)
                blocks.append(_dot_nt(qh, kh))
            A = jnp.concatenate(blocks, axis=0)

            diag = jnp.zeros((C, 128), F32)
            for jj in range(SB):
                ksel = jnp.concatenate(
                    [jnp.broadcast_to(k[s * SB + jj:s * SB + jj + 1], (SB, DK)) for s in range(C // SB)], axis=0)
                bsel = jnp.concatenate(
                    [jnp.broadcast_to(b[s * SB + jj:s * SB + jj + 1], (SB, DK)) for s in range(C // SB)], axis=0)
                e = jnp.exp2(jnp.where(sub_row >= jj, b - bsel, NEG))
                col = _dot(q * e * ksel, ones_ref[...])
                diag = jnp.where(lane128 % SB == jj, col, diag)
            A = A + jnp.where(diag_mask, diag, 0.0)[:, 0:C]

            o = o + _dot(A, v)
            on = o * lax.rsqrt(jnp.mean(o * o, axis=-1, keepdims=True) + NORM_EPS) * nw_ref[...]
            gate = g_ref[rs, vs]
            o_ref[rs, vs] = (on * (gate * jax.nn.sigmoid(gate))).astype(BF16)
            state_ref[hd] = S * jnp.exp2(b_last) + _dot_tn(v, k * jnp.exp2(b_last - b))
        return carry

    lax.fori_loop(0, tt // C, chunk, 0)


def _gla(ug, hn, w_gd, gk_up, gk_b, norm_w, tril, ones):
    tp = ug.shape[0]
    tt = HEAD
    tb = _time_block(tp // tt)
    return pl.pallas_call(
        functools.partial(_gla_kernel, tt=tt),
        grid=(tp // tt,),
        in_specs=[pl.BlockSpec((tt, GLA_KW), lambda i: (tb(i), 0)),
                  pl.BlockSpec((tt, GLA_KW), lambda i: (tb(i), 1)),
                  pl.BlockSpec((tt, GLA_VW), lambda i: (tb(i), 1)),
                  pl.BlockSpec((tt, GLA_VW), lambda i: (tb(i), 2)),
                  pl.BlockSpec((tt, D_MODEL), lambda i: (tb(i), 0)),
                  pl.BlockSpec((D_MODEL, 128), lambda i: (0, 0)),
                  pl.BlockSpec((128, GLA_KW), lambda i: (0, 0)),
                  pl.BlockSpec((1, GLA_KW), lambda i: (0, 0)),
                  pl.BlockSpec((1, GLA_DV), lambda i: (0, 0)),
                  pl.BlockSpec((GLA_CHUNK, GLA_CHUNK), lambda i: (0, 0)),
                  pl.BlockSpec((GLA_DK, 128), lambda i: (0, 0))],
        out_specs=pl.BlockSpec((tt, GLA_VW), lambda i: (tb(i), 0)),
        out_shape=jax.ShapeDtypeStruct((tp, GLA_VW), BF16),
        scratch_shapes=[pltpu.VMEM((GLA_HEADS, GLA_DV, GLA_DK), F32),
                        pltpu.VMEM((tt, GLA_KW), F32)],
        compiler_params=_params(("arbitrary",)),
        name="gla",
    )(ug, ug, ug, ug, hn, w_gd, gk_up, gk_b, norm_w, tril, ones)


def _block_diag_groups(w):
    nb = LRU_GROUP // LRU_BLOCK
    w4 = w.reshape(-1, nb, LRU_BLOCK, LRU_BLOCK)
    out = jnp.einsum("gaij,ab->gaibj", w4, jnp.eye(nb, dtype=w.dtype))
    return out.reshape(-1, LRU_GROUP, LRU_GROUP)


def _row(v):
    return v.reshape(1, -1).astype(F32)


def _ffn(h, hn, layer, w_gate, w_val, conv_w, conv_b, w_down, gpost, gnext):
    cw = jnp.zeros((8, D_FF), F32).at[:FFN_CONV].set(conv_w[layer])
    act = _ffn_up(hn, w_gate, w_val, layer, cw, _row(conv_b[layer]))
    tiles = (416, 256) if gnext is not None else (512, 256)
    return _mm_res([act], [w_down[layer].astype(BF16)], h, gpost, gnext, tiles, f"ffn_down{layer}")


def kernel(x, meta_tokens, mix_pre_norm, mix_post_norm, ffn_pre_norm, ffn_post_norm, ab_w_in, lru_conv_w, lru_conv_b, lru_gate_a_w, lru_gate_a_b, lru_gate_x_w, lru_gate_x_b, lru_lambda, rwkv_shift_mu, rwkv_w0, rwkv_w2, rwkv_a0, rwkv_a2, rwkv_g2, rwkv_k_k, rwkv_k_a, rwkv_r_k, rwkv_ln_w, rwkv_ln_b, ab_w_out, gla_w_in, gla_gk_up, gla_gk_b, gla_norm_w, gla_w_out, ffn_w_gate, ffn_w_val, ffn_conv_w, ffn_conv_b, ffn_w_down):
    batch, seq, _ = x.shape
    assert batch == 1 and seq % HEAD == 0
    W = RWKV_WIDTH
    head = jnp.concatenate([jnp.zeros((PAD, D_MODEL), F32), meta_tokens.astype(F32)], axis=0)
    h, hn = _embed(head, x[0], _row(mix_pre_norm[0]))

    u = _mm(hn, ab_w_in, ab_w_in.shape[2], 768, "ab_in")
    cw = jnp.zeros((8, LRU_WIDTH), F32).at[:LRU_CONV].set(lru_conv_w[0])
    lru_out = _lru(u, cw, _row(lru_conv_b[0]),
                   _block_diag_groups(lru_gate_a_w[0]).astype(BF16), _row(lru_gate_a_b[0]),
                   _block_diag_groups(lru_gate_x_w[0]).astype(BF16), _row(lru_gate_x_b[0]),
                   _row(lru_lambda[0]))
    half = RWKV_LORA // 2
    wwa = jnp.zeros((RWKV_LORA, 2 * W), F32)
    wwa = wwa.at[:half, :W].set(rwkv_w2[0]).at[half:, W:].set(rwkv_a2[0]).astype(BF16)
    seg = jnp.kron(jnp.eye(RWKV_GROUP // RWKV_HEAD_DIM, dtype=F32),
                   jnp.ones((RWKV_HEAD_DIM, RWKV_HEAD_DIM), F32)).astype(BF16)
    mu = jnp.zeros((8, W), F32).at[:3].set(rwkv_shift_mu[0, :3 * W].reshape(3, W))
    chunk_tril = jnp.kron(jnp.eye(HEAD // RWKV_CHUNK, dtype=F32),
                          jnp.tril(jnp.ones((RWKV_CHUNK, RWKV_CHUNK), F32))).astype(BF16)
    rw_out = _rwkv(u, mu, _row(rwkv_shift_mu[0, 3 * W:]), wwa, _row(rwkv_w0[0]), _row(rwkv_a0[0]),
                   rwkv_g2[0].astype(BF16), _row(rwkv_k_k[0]), _row(rwkv_k_a[0]), _row(rwkv_r_k[0]),
                   _row(rwkv_ln_w[0]), _row(rwkv_ln_b[0]), seg, chunk_tril)
    w_out = ab_w_out[0].astype(BF16)
    h, hn = _mm_res([lru_out, rw_out], [w_out[:LRU_WIDTH], w_out[LRU_WIDTH:]], h,
                    _row(mix_post_norm[0]), _row(ffn_pre_norm[0]), (640, 256), "ab_out")
    h, hn = _ffn(h, hn, 0, ffn_w_gate, ffn_w_val, ffn_conv_w, ffn_conv_b, ffn_w_down,
                 _row(ffn_post_norm[0]), _row(mix_pre_norm[1]))

    ug = _mm(hn, gla_w_in, GLA_MAIN, 768, "gla_in")
    w_gd = jnp.zeros((D_MODEL, 128), F32).at[:, :GLA_GATE_RANK].set(gla_w_in[0, :, GLA_MAIN:]).astype(BF16)
    up = jnp.zeros((128, GLA_KW), F32).at[:GLA_GATE_RANK].set(gla_gk_up[0]).astype(BF16)
    tril = jnp.tril(jnp.ones((GLA_CHUNK, GLA_CHUNK), F32)).astype(BF16)
    ones = jnp.ones((GLA_DK, 128), BF16)
    o = _gla(ug, hn, w_gd, up, _row(gla_gk_b[0]), _row(gla_norm_w[0]), tril, ones)
    h, hn = _mm_res([o], [gla_w_out[0].astype(BF16)], h,
                    _row(mix_post_norm[1]), _row(ffn_pre_norm[1]), (640, 256), "gla_out")
    out, _ = _ffn(h, hn, 1, ffn_w_gate, ffn_w_val, ffn_conv_w, ffn_conv_b, ffn_w_down,
                  _row(ffn_post_norm[1]), None)
    return out[None]
```

```python
import functools
import math

import jax
import jax.numpy as jnp
from jax import lax
from jax.experimental import pallas as pl
from jax.experimental.pallas import tpu as pltpu

F32 = jnp.float32
BF16 = jnp.bfloat16

D_MODEL = 2048
N_META = 16
NORM_EPS = 1e-6
HEAD = 256
PAD = HEAD - N_META
LRU_WIDTH = 1024
LRU_BLOCK = 64
LRU_GROUP = 256
LRU_CONV = 4
LRU_C = 8.0
RWKV_WIDTH = 1024
RWKV_HEAD_DIM = 64
RWKV_LORA = 128
RWKV_GN_EPS = 64e-5
RWKV_CHUNK = 64
RWKV_GROUP = 128
GLA_HEADS = 4
GLA_DK = 256
GLA_DV = 512
GLA_KW = GLA_HEADS * GLA_DK
GLA_VW = GLA_HEADS * GLA_DV
GLA_GATE_RANK = 16
GLA_GATE_NORM = 16.0
GLA_CHUNK = 64
GLA_SUB = 8
GLA_BATCH = 4
GLA_MAIN = 2 * GLA_KW + 2 * GLA_VW
D_FF = 5632
FFN_CONV = 3
LOG2E = 1.4426950408889634
VMEM_LIMIT = 56 * 1024 * 1024


def _pick(n, candidates):
    for c in candidates:
        if n % c == 0:
            return c
    raise ValueError(f"no tile for {n} in {candidates}")


def _params(sem):
    return pltpu.CompilerParams(dimension_semantics=sem, vmem_limit_bytes=VMEM_LIMIT)


def _dot(a, b):
    return jnp.dot(a.astype(BF16), b.astype(BF16), preferred_element_type=F32)


def _dot_nt(a, b):
    return lax.dot_general(a.astype(BF16), b.astype(BF16), (((1,), (1,)), ((), ())),
                           preferred_element_type=F32)


def _dot_tn(a, b):
    return lax.dot_general(a.astype(BF16), b.astype(BF16), (((0,), (0,)), ((), ())),
                           preferred_element_type=F32)


def _dot_split(a, b_bf16):
    hi = a.astype(BF16)
    lo = (a - hi.astype(F32)).astype(BF16)
    return (jnp.dot(hi, b_bf16, preferred_element_type=F32)
            + jnp.dot(lo, b_bf16, preferred_element_type=F32))


def _softplus(x):
    return jnp.maximum(x, 0.0) + jnp.log1p(jnp.exp(-jnp.abs(x)))


def _expm1(x):
    u = jnp.exp(x)
    d = u - 1.0
    return jnp.where(d == 0.0, x, d * x / jnp.log(jnp.where(d == 0.0, 2.0, u)))


def _rms(x):
    return x * lax.rsqrt(jnp.mean(x * x, axis=-1, keepdims=True) + NORM_EPS)


def _row_ids(shape, base):
    return lax.broadcasted_iota(jnp.int32, shape, 0) + base


def _shift_rows(x, s, prev8):
    rolled = pltpu.roll(x, s, 0)
    rows = lax.broadcasted_iota(jnp.int32, prev8.shape, 0)
    top = jnp.where(rows < s, pltpu.roll(prev8, s, 0), rolled[0:8])
    return jnp.concatenate([top, rolled[8:]], axis=0)


def _time_block(nb):
    return lambda i: ((i + nb - 1) % nb)


def _embed_kernel(head_ref, x_ref, g_ref, h_ref, hn_ref):
    i = pl.program_id(0)
    xb = jnp.where(i == pl.num_programs(0) - 1, head_ref[...], x_ref[...])
    h_ref[...] = xb
    hn_ref[...] = (_rms(xb) * g_ref[...]).astype(BF16)


def _embed(head, x2d, gain):
    seq = x2d.shape[0]
    tp = seq + HEAD
    nb = tp // HEAD
    blk = pl.BlockSpec((HEAD, D_MODEL), lambda i: (i, 0))
    return pl.pallas_call(
        _embed_kernel,
        grid=(nb,),
        in_specs=[pl.BlockSpec((HEAD, D_MODEL), lambda i: (0, 0)),
                  pl.BlockSpec((HEAD, D_MODEL), lambda i: (jnp.minimum(i, nb - 2), 0)),
                  pl.BlockSpec((1, D_MODEL), lambda i: (0, 0))],
        out_specs=[blk, blk],
        out_shape=[jax.ShapeDtypeStruct((tp, D_MODEL), F32),
                   jax.ShapeDtypeStruct((tp, D_MODEL), BF16)],
        compiler_params=_params(("arbitrary",)),
        name="embed",
    )(head, x2d, gain)


def _mm_kernel(x_ref, w_ref, o_ref):
    o_ref[...] = jnp.dot(x_ref[...], w_ref[...].astype(BF16), preferred_element_type=F32)


def _mm(x, w3, n, tn, name):
    tp, k = x.shape
    tm = _pick(tp, (1280, 768, 256))
    return pl.pallas_call(
        _mm_kernel,
        grid=(tp // tm, n // tn),
        in_specs=[pl.BlockSpec((tm, k), lambda i, j: (i, 0)),
                  pl.BlockSpec((None, k, tn), lambda i, j: (0, 0, j))],
        out_specs=pl.BlockSpec((tm, tn), lambda i, j: (i, j)),
        out_shape=jax.ShapeDtypeStruct((tp, n), F32),
        compiler_params=_params(("arbitrary", "arbitrary")),
        name=name,
    )(x, w3)


def _mm_res_kernel(*refs, n_x, tm, tn, seq, has_next):
    x_refs = refs[:n_x]
    w_refs = refs[n_x:2 * n_x]
    h_ref, gpost_ref = refs[2 * n_x:2 * n_x + 2]
    rest = refs[2 * n_x + 2:]
    if has_next:
        gnext_ref, ho_ref, hn_ref, acc_a, acc_b = rest
    else:
        ho_ref, acc_a, acc_b = rest
    s = pl.program_id(0)
    nt = D_MODEL // tn

    @pl.when(s == 0)
    def _():
        acc_b[...] = jnp.zeros_like(acc_b)

    def step(cur_ref, prev_ref):
        def matmul_cols(t):
            cs = slice(t * tn, (t + 1) * tn)
            part = jnp.dot(x_refs[0][...], w_refs[0][:, cs], preferred_element_type=F32)
            for xr, wr in zip(x_refs[1:], w_refs[1:]):
                part = part + jnp.dot(xr[...], wr[:, cs], preferred_element_type=F32)
            cur_ref[t] = part

        matmul_cols(0)
        ssq = jnp.zeros((tm, 1), F32)
        for t in range(nt):
            m = prev_ref[t]
            ssq = ssq + jnp.sum(m * m, axis=-1, keepdims=True)
        scale = lax.rsqrt(ssq * (1.0 / D_MODEL) + NORM_EPS)
        rows = _row_ids((tm, 1), (s - 1) * tm)
        scale = jnp.where((rows < seq) | (rows >= seq + PAD), scale, 0.0)
        matmul_cols(1)
        ssq2 = jnp.zeros((tm, 1), F32)
        for t in range(nt):
            cs = slice(t * tn, (t + 1) * tn)
            hnew = h_ref[:, cs] + prev_ref[t] * scale * gpost_ref[:, cs]
            ho_ref[:, cs] = hnew
            if has_next:
                ssq2 = ssq2 + jnp.sum(hnew * hnew, axis=-1, keepdims=True)
        matmul_cols(2)
        if has_next:
            scale2 = lax.rsqrt(ssq2 * (1.0 / D_MODEL) + NORM_EPS)
            for t in range(nt):
                cs = slice(t * tn, (t + 1) * tn)
                hn_ref[:, cs] = (ho_ref[:, cs] * scale2 * gnext_ref[:, cs]).astype(BF16)
        for t in range(3, nt):
            matmul_cols(t)

    @pl.when(s % 2 == 0)
    def _():
        step(acc_a, acc_b)

    @pl.when(s % 2 == 1)
    def _():
        step(acc_b, acc_a)


def _mm_res(xs, ws, h, gpost, gnext, tm_candidates, name):
    tp = h.shape[0]
    seq = tp - HEAD
    kdim = xs[0].shape[1]
    has_next = gnext is not None
    rows_out = tp if has_next else seq
    tm = _pick(rows_out, tm_candidates)
    tn = 512
    n = rows_out // tm
    row = pl.BlockSpec((tm, D_MODEL), lambda s: (jnp.maximum(s - 1, 0), 0))
    gain = pl.BlockSpec((1, D_MODEL), lambda s: (0, 0))
    in_specs = ([pl.BlockSpec((tm, kdim), lambda s: (jnp.minimum(s, n - 1), 0)) for _ in xs]
                + [pl.BlockSpec((kdim, D_MODEL), lambda s: (0, 0), pipeline_mode=pl.Buffered(1)) for _ in ws]
                + [row, gain] + ([gain] if has_next else []))
    out_specs = [row] + ([row] if has_next else [])
    out_shape = [jax.ShapeDtypeStruct((rows_out, D_MODEL), F32)]
    if has_next:
        out_shape.append(jax.ShapeDtypeStruct((tp, D_MODEL), BF16))
    args = list(xs) + list(ws) + [h, gpost] + ([gnext] if has_next else [])
    out = pl.pallas_call(
        functools.partial(_mm_res_kernel, n_x=len(xs), tm=tm, tn=tn, seq=seq, has_next=has_next),
        grid=(n + 1,),
        in_specs=in_specs,
        out_specs=out_specs,
        out_shape=out_shape,
        scratch_shapes=[pltpu.VMEM((D_MODEL // tn, tm, tn), F32)] * 2,
        compiler_params=_params(("arbitrary",)),
        name=name,
    )(*args)
    return out if has_next else (out[0], None)


def _ffn_up_kernel(x_ref, xh_ref, wg_ref, wv_ref, cw_ref, cb_ref, o_ref):
    x = x_ref[...]
    gate_ext = jnp.dot(jnp.concatenate([xh_ref[...], x], axis=0), wg_ref[...].astype(BF16),
                       preferred_element_type=F32)
    gate = gate_ext[16:]
    halo = gate_ext[8:16]
    val = jnp.dot(x, wv_ref[...].astype(BF16), preferred_element_type=F32)
    cw = cw_ref[...]
    conv = (cw[2:3] * gate + cw[1:2] * _shift_rows(gate, 1, halo)
            + cw[0:1] * _shift_rows(gate, 2, halo) + cb_ref[...])
    o_ref[...] = (jax.nn.gelu(conv) * val).astype(BF16)


def _ffn_up(hn, w_gate, w_val, layer, conv_w, conv_b):
    tp = hn.shape[0]
    tm = _pick(tp, (1280, 768, 256))
    tn = 512
    halo_blocks = tm // 16
    nhalo = tp // 16
    wspec = pl.BlockSpec((None, D_MODEL, tn), lambda i, j: (layer, 0, j))
    return pl.pallas_call(
        _ffn_up_kernel,
        grid=(tp // tm, D_FF // tn),
        in_specs=[pl.BlockSpec((tm, D_MODEL), lambda i, j: (i, 0)),
                  pl.BlockSpec((16, D_MODEL), lambda i, j: ((i * halo_blocks + nhalo - 1) % nhalo, 0)),
                  wspec, wspec,
                  pl.BlockSpec((8, tn), lambda i, j: (0, j)),
                  pl.BlockSpec((1, tn), lambda i, j: (0, j))],
        out_specs=pl.BlockSpec((tm, tn), lambda i, j: (i, j)),
        out_shape=jax.ShapeDtypeStruct((tp, D_FF), BF16),
        compiler_params=_params(("arbitrary", "arbitrary")),
        name="ffn_up",
    )(hn, hn, w_gate, w_val, conv_w, conv_b)


def _lru_kernel(lx_ref, ly_ref, cw_ref, cb_ref, wa_ref, ba_ref, wx_ref, bx_ref, lam_ref,
                o_ref, halo_ref, state_ref, *, tt):
    i = pl.program_id(0)

    @pl.when(i == 0)
    def _():
        halo_ref[...] = jnp.zeros_like(halo_ref)
        state_ref[...] = jnp.zeros_like(state_ref)

    lx = lx_ref[...]
    prev = halo_ref[...]
    cw = cw_ref[...]
    xc = (cw[3:4] * lx + cw[2:3] * _shift_rows(lx, 1, prev) + cw[1:2] * _shift_rows(lx, 2, prev)
          + cw[0:1] * _shift_rows(lx, 3, prev) + cb_ref[...])
    halo_ref[...] = lx[tt - 8:tt]

    xb = xc.astype(BF16)
    ga, gx = [], []
    for g in range(LRU_WIDTH // LRU_GROUP):
        blk = xb[:, g * LRU_GROUP:(g + 1) * LRU_GROUP]
        ga.append(jnp.dot(blk, wa_ref[g], preferred_element_type=F32))
        gx.append(jnp.dot(blk, wx_ref[g], preferred_element_type=F32))
    r = jax.nn.sigmoid(jnp.concatenate(ga, axis=1) + ba_ref[...])
    gi = jax.nn.sigmoid(jnp.concatenate(gx, axis=1) + bx_ref[...])
    log_a = (-LRU_C) * r * _softplus(-lam_ref[...])
    a = jnp.exp(log_a)
    rows = _row_ids(a.shape, i * tt)
    b = jnp.where(rows >= PAD, jnp.sqrt(-_expm1(2.0 * log_a)) * (gi * xc), 0.0)

    local = lax.broadcasted_iota(jnp.int32, a.shape, 0)
    d = 1
    while d < tt:
        if d < 8:
            a_sh = jnp.where(local < d, 1.0, pltpu.roll(a, d, 0))
            b_sh = jnp.where(local < d, 0.0, pltpu.roll(b, d, 0))
        else:
            a_sh = jnp.concatenate([jnp.ones((d, LRU_WIDTH), F32), a[0:tt - d]], axis=0)
            b_sh = jnp.concatenate([jnp.zeros((d, LRU_WIDTH), F32), b[0:tt - d]], axis=0)
        b = b + a * b_sh
        a = a * a_sh
        d *= 2
    h = b + a * state_ref[...]
    state_ref[...] = h[tt - 1:tt]
    o_ref[...] = (h * jax.nn.gelu(ly_ref[...])).astype(BF16)


def _lru(u, conv_w, conv_b, wa, ba, wx, bx, lam):
    tp = u.shape[0]
    tt = HEAD
    tb = _time_block(tp // tt)
    vec = pl.BlockSpec((1, LRU_WIDTH), lambda i: (0, 0))
    wspec = pl.BlockSpec((LRU_WIDTH // LRU_GROUP, LRU_GROUP, LRU_GROUP), lambda i: (0, 0, 0))
    return pl.pallas_call(
        functools.partial(_lru_kernel, tt=tt),
        grid=(tp // tt,),
        in_specs=[pl.BlockSpec((tt, LRU_WIDTH), lambda i: (tb(i), 0)),
                  pl.BlockSpec((tt, LRU_WIDTH), lambda i: (tb(i), 1)),
                  pl.BlockSpec((8, LRU_WIDTH), lambda i: (0, 0)), vec,
                  wspec, vec, wspec, vec, vec],
        out_specs=pl.BlockSpec((tt, LRU_WIDTH), lambda i: (tb(i), 0)),
        out_shape=jax.ShapeDtypeStruct((tp, LRU_WIDTH), BF16),
        scratch_shapes=[pltpu.VMEM((8, LRU_WIDTH), F32), pltpu.VMEM((1, LRU_WIDTH), F32)],
        compiler_params=_params(("arbitrary",)),
        name="rglru",
    )(u, u, conv_w, conv_b, wa, ba, wx, bx, lam)


def _rwkv_kernel(r_ref, k_ref, vin_ref, lo_ref, mu_ref, mul_ref, wwa_ref, w0_ref, a0_ref, g2_ref,
                 kk_ref, ka_ref, rk_ref, lnw_ref, lnb_ref, seg_ref, tril_ref, o_ref,
                 last_ref, lastl_ref, state_ref, rt_ref, kt_ref, kb_ref, pb_ref, v_ref, gam_ref, y_ref,
                 x_ref, bv_ref, qkv_ref, qp_ref, *, tt):
    i = pl.program_id(0)
    C = RWKV_CHUNK
    G = RWKV_GROUP
    W = RWKV_WIDTH
    NG = W // G

    @pl.when(i == 0)
    def _():
        last_ref[...] = jnp.zeros_like(last_ref)
        lastl_ref[...] = jnp.zeros_like(lastl_ref)
        state_ref[...] = jnp.zeros_like(state_ref)

    def token_shift(x, mu, last):
        return x + (_shift_rows(x, 1, jnp.tile(last, (8, 1))) - x) * mu

    r_in, k_in, v_in, lo_in = r_ref[...], k_ref[...], vin_ref[...], lo_ref[...]
    r = token_shift(r_in, mu_ref[0:1], last_ref[0:1])
    k = token_shift(k_in, mu_ref[1:2], last_ref[1:2])
    v = token_shift(v_in, mu_ref[2:3], last_ref[2:3])
    lo = token_shift(lo_in, mul_ref[...], lastl_ref[...])
    last_ref[0:1] = r_in[tt - 1:tt]
    last_ref[1:2] = k_in[tt - 1:tt]
    last_ref[2:3] = v_in[tt - 1:tt]
    lastl_ref[...] = lo_in[tt - 1:tt]

    lora = lo[:, 0:RWKV_LORA]
    lane = lax.broadcasted_iota(jnp.int32, lora.shape, 1)
    lora = jnp.where(lane < RWKV_LORA // 2, jnp.tanh(lora), lora)
    wa = _dot(lora, wwa_ref[...])
    ell = (-math.exp(-0.5) * LOG2E) * jax.nn.sigmoid(w0_ref[...] + wa[:, 0:W])
    a = jax.nn.sigmoid(a0_ref[...] + wa[:, W:2 * W])
    g = _dot(jax.nn.sigmoid(lo[:, RWKV_LORA:]), g2_ref[...])

    seg = seg_ref[...]

    def head_sums(x):
        return jnp.concatenate([_dot(x[:, t * G:(t + 1) * G], seg) for t in range(NG)], axis=1)

    kk = k * kk_ref[...]
    kk = kk / jnp.maximum(jnp.sqrt(head_sums(kk * kk)), 1e-12)
    rows = _row_ids(k.shape, i * tt)
    k = jnp.where(rows >= PAD, k * (1.0 + (a - 1.0) * ka_ref[...]), 0.0)
    bonus = head_sums(r * k * rk_ref[...]) * v

    b = _cumsum_rows(ell, tril_ref[...])
    eb = jnp.exp2(-b)
    gam = jnp.exp2(b)
    rt_ref[...] = r * gam
    kt_ref[...] = kk * jnp.exp2(b - ell)
    kb_ref[...] = k * eb
    pb_ref[...] = kk * a * eb
    v_ref[...] = v
    gam_ref[...] = gam

    crow = lax.broadcasted_iota(jnp.int32, (C, G), 0)
    ccol = lax.broadcasted_iota(jnp.int32, (C, G), 1) % C
    strict = ccol < crow
    incl = ccol <= crow
    eye = jnp.where(ccol == crow, 1.0, 0.0)
    groups = range(NG)

    def bd(x):
        xb = x.astype(BF16)
        return jnp.concatenate([xb, xb], axis=0) * seg

    chunks = range(tt // C)

    blocks = [(slice(c * C, (c + 1) * C), slice(gi * G, (gi + 1) * G)) for c in chunks for gi in groups]

    keys = [jnp.concatenate([bd(pb_ref[rs, ls]), bd(kb_ref[rs, ls])], axis=0) for rs, ls in blocks]
    sc = [_dot_nt(jnp.concatenate([kt_ref[rs, ls], rt_ref[rs, ls]], axis=0), keys[n])
          for n, (rs, ls) in enumerate(blocks)]
    A = [jnp.where(strict, m[0:C, 0:G], 0.0) for m in sc]
    for n, (rs, ls) in enumerate(blocks):
        both = jnp.concatenate([jnp.where(strict, sc[n][0:C, G:2 * G], 0.0),
                                jnp.where(incl, sc[n][C:2 * C, G:2 * G], 0.0)], axis=0)
        prod = _dot(both, bd(v_ref[rs, ls]))
        bv_ref[rs, ls] = prod[0:C]
        qkv_ref[rs, ls] = prod[C:2 * C]
        qp_ref[rs, ls] = jnp.where(incl, sc[n][C:2 * C, 0:G], 0.0)
    X = [eye - m for m in A]
    P = [_dot(m, bd(m)) for m in A]
    n_pow = 2
    while n_pow < C:
        n_pow *= 2
        if n_pow < C:
            xp = [_dot(jnp.concatenate([x, p], axis=0), bd(p)) for x, p in zip(X, P)]
            X = [x + m[0:C] for x, m in zip(X, xp)]
            P = [m[C:2 * C] for m in xp]
        else:
            X = [x + _dot(x, bd(p)) for x, p in zip(X, P)]
    for n, (rs, ls) in enumerate(blocks):
        x_ref[rs, ls] = X[n]

    S = [state_ref[gi] for gi in groups]
    for c in chunks:
        rs = slice(c * C, (c + 1) * C)
        cols = [slice(gi * G, (gi + 1) * G) for gi in groups]
        krs = [_dot_nt(jnp.concatenate([kt_ref[rs, ls], rt_ref[rs, ls]], axis=0), S[gi])
               for gi, ls in enumerate(cols)]
        rts = [m[C:2 * C] for m in krs]
        U = [_dot(x_ref[rs, ls], bd(krs[gi][0:C] + bv_ref[rs, ls])) for gi, ls in enumerate(cols)]
        gg = [gam_ref[(c + 1) * C - 1:(c + 1) * C, ls] for ls in cols]
        upd = [_dot_tn(jnp.concatenate([v_ref[rs, ls], U[gi]], axis=0),
                       jnp.concatenate([kb_ref[rs, ls] * gg[gi], -(pb_ref[rs, ls] * gg[gi])], axis=0))
               for gi, ls in enumerate(cols)]
        for gi, ls in enumerate(cols):
            y_ref[rs, ls] = rts[gi] + qkv_ref[rs, ls] - _dot(qp_ref[rs, ls], bd(U[gi]))
        S = [S[gi] * gg[gi] + upd[gi] * seg for gi in groups]
    for gi in groups:
        state_ref[gi] = S[gi]

    y = y_ref[...]
    inv_n = 1.0 / RWKV_HEAD_DIM
    mu = head_sums(y) * inv_n
    dlt = y - mu
    var = head_sums(dlt * dlt) * inv_n
    yn = dlt * lax.rsqrt(var + RWKV_GN_EPS) * lnw_ref[...] + lnb_ref[...]
    o_ref[...] = ((yn + bonus) * g).astype(BF16)


def _rwkv(u, mu, mu_lora, wwa, w0, a0, g2, k_k, k_a, r_k, ln_w, ln_b, seg, tril):
    tp = u.shape[0]
    tt = HEAD
    tb = _time_block(tp // tt)
    W = RWKV_WIDTH
    G = RWKV_GROUP
    L2 = 2 * RWKV_LORA
    vec = pl.BlockSpec((1, W), lambda i: (0, 0))
    first = 2 * LRU_WIDTH // W
    big = pltpu.VMEM((tt, W), F32)
    return pl.pallas_call(
        functools.partial(_rwkv_kernel, tt=tt),
        grid=(tp // tt,),
        in_specs=[pl.BlockSpec((tt, W), lambda i: (tb(i), first)),
                  pl.BlockSpec((tt, W), lambda i: (tb(i), first + 1)),
                  pl.BlockSpec((tt, W), lambda i: (tb(i), first + 2)),
                  pl.BlockSpec((tt, L2), lambda i: (tb(i), (first + 3) * W // L2)),
                  pl.BlockSpec((8, W), lambda i: (0, 0)),
                  pl.BlockSpec((1, L2), lambda i: (0, 0)),
                  pl.BlockSpec((RWKV_LORA, 2 * W), lambda i: (0, 0)), vec, vec,
                  pl.BlockSpec((RWKV_LORA, W), lambda i: (0, 0)), vec, vec, vec, vec, vec,
                  pl.BlockSpec((G, G), lambda i: (0, 0)),
                  pl.BlockSpec((tt, tt), lambda i: (0, 0))],
        out_specs=pl.BlockSpec((tt, W), lambda i: (tb(i), 0)),
        out_shape=jax.ShapeDtypeStruct((tp, W), BF16),
        scratch_shapes=[pltpu.VMEM((8, W), F32), pltpu.VMEM((1, L2), F32),
                        pltpu.VMEM((W // G, G, G), F32)] + [big] * 11,
        compiler_params=_params(("arbitrary",)),
        name="rwkv7",
    )(u, u, u, u, mu, mu_lora, wwa, w0, a0, g2, k_k, k_a, r_k, ln_w, ln_b, seg, tril)


def _cumsum_rows(x, tril_bf16):
    hi = x.astype(BF16)
    lo = (x - hi.astype(F32)).astype(BF16)
    return (jnp.dot(tril_bf16, hi, preferred_element_type=F32)
            + jnp.dot(tril_bf16, lo, preferred_element_type=F32))


def _gla_kernel(q_ref, k_ref, v_ref, g_ref, hn_ref, wgd_ref, up_ref, gb_ref, nw_ref, tril_ref, ones_ref,
                o_ref, state_ref, gk_ref, qt_ref, kt_ref, a_ref, dec_ref, *, tt):
    i = pl.program_id(0)
    C = GLA_CHUNK
    SB = GLA_SUB
    DK = GLA_DK
    DV = GLA_DV
    NEG = -jnp.inf

    @pl.when(i == 0)
    def _():
        state_ref[...] = jnp.zeros_like(state_ref)

    gd = jnp.dot(hn_ref[...], wgd_ref[...], preferred_element_type=F32)
    z = _dot(gd, up_ref[...]) + gb_ref[...]
    gk_ref[...] = -_softplus(-z) * (LOG2E / GLA_GATE_NORM)

    crow = lax.broadcasted_iota(jnp.int32, (C, DK), 0)
    sub_row = crow % SB
    lane128 = lax.broadcasted_iota(jnp.int32, (C, 128), 1)
    row128 = lax.broadcasted_iota(jnp.int32, (C, 128), 0)
    diag_mask = (lane128 // SB == row128 // SB) & (lane128 < C)
    scale = DK ** -0.5

    heads = range(GLA_HEADS)
    chunks = range(tt // C)
    blocks = [(c, hd) for c in chunks for hd in heads]
    nsub = C // SB

    def row_sel(x, jj):
        return jnp.concatenate(
            [jnp.broadcast_to(x[s * SB + jj:s * SB + jj + 1], (SB, DK)) for s in range(nsub)], axis=0)

    for start in range(0, len(blocks), GLA_BATCH):
        batch = blocks[start:start + GLA_BATCH]
        idx = [(slice(c * C, (c + 1) * C), slice(hd * DK, (hd + 1) * DK)) for c, hd in batch]
        b = [_cumsum_rows(gk_ref[rs, ks], tril_ref[...]) for rs, ks in idx]
        q = [q_ref[rs, ks] * scale for rs, ks in idx]
        k = [jnp.where((crow + (i * tt + c * C)) >= PAD, k_ref[rs, ks], 0.0)
             for (c, _), (rs, ks) in zip(batch, idx)]
        for n, ((c, _), (rs, ks)) in enumerate(zip(batch, idx)):
            b_last = b[n][C - 1:C]
            qt_ref[rs, ks] = q[n] * jnp.exp2(b[n])
            kt_ref[rs, ks] = k[n] * jnp.exp2(b_last - b[n])
            dec_ref[c:c + 1, ks] = jnp.exp2(b_last)

        pieces = [[jnp.zeros((SB, 128), F32)] for _ in batch]
        for sb in range(1, nsub):
            r0 = sb * SB
            for n in range(len(batch)):
                ref_b = b[n][r0 - 1:r0]
                qh = q[n][r0:r0 + SB] * jnp.exp2(b[n][r0:r0 + SB] - ref_b)
                kh = k[n][0:r0] * jnp.exp2(ref_b - b[n][0:r0])
                kh = jnp.concatenate([kh, jnp.zeros((128 - r0, DK), F32)], axis=0)
                pieces[n].append(_dot_nt(qh, kh))

        diag = [jnp.zeros((C, 128), F32) for _ in batch]
        for jj in range(SB):
            for n in range(len(batch)):
                e = jnp.exp2(jnp.where(sub_row >= jj, b[n] - row_sel(b[n], jj), NEG))
                col = _dot(q[n] * e * row_sel(k[n], jj), ones_ref[...])
                diag[n] = jnp.where(lane128 % SB == jj, col, diag[n])
        for n, (c, hd) in enumerate(batch):
            a_ref[c * C:(c + 1) * C, hd * 128:(hd + 1) * 128] = (
                jnp.concatenate(pieces[n], axis=0) + jnp.where(diag_mask, diag[n], 0.0))

    for c in chunks:
        rs = slice(c * C, (c + 1) * C)
        valid_v = (lax.broadcasted_iota(jnp.int32, (C, DV), 0) + (i * tt + c * C)) >= PAD
        ksl = [slice(hd * DK, (hd + 1) * DK) for hd in heads]
        vsl = [slice(hd * DV, (hd + 1) * DV) for hd in heads]
        v = [jnp.where(valid_v, v_ref[rs, vs], 0.0) for vs in vsl]
        o = [_dot_nt(qt_ref[rs, ksl[hd]], state_ref[hd]) for hd in heads]
        o = [o[hd] + _dot(a_ref[rs, hd * 128:(hd + 1) * 128],
                          jnp.concatenate([v[hd], jnp.zeros((128 - C, DV), F32)], axis=0))
             for hd in heads]
        for hd in heads:
            on = o[hd] * lax.rsqrt(jnp.mean(o[hd] * o[hd], axis=-1, keepdims=True) + NORM_EPS) * nw_ref[...]
            gate = g_ref[rs, vsl[hd]]
            o_ref[rs, vsl[hd]] = (on * (gate * jax.nn.sigmoid(gate))).astype(BF16)
        upd = [_dot_tn(v[hd], kt_ref[rs, ksl[hd]]) for hd in heads]
        for hd in heads:
            state_ref[hd] = state_ref[hd] * dec_ref[c:c + 1, ksl[hd]] + upd[hd]


def _gla(ug, hn, w_gd, gk_up, gk_b, norm_w, tril, ones):
    tp = ug.shape[0]
    tt = HEAD
    tb = _time_block(tp // tt)
    return pl.pallas_call(
        functools.partial(_gla_kernel, tt=tt),
        grid=(tp // tt,),
        in_specs=[pl.BlockSpec((tt, GLA_KW), lambda i: (tb(i), 0)),
                  pl.BlockSpec((tt, GLA_KW), lambda i: (tb(i), 1)),
                  pl.BlockSpec((tt, GLA_VW), lambda i: (tb(i), 1)),
                  pl.BlockSpec((tt, GLA_VW), lambda i: (tb(i), 2)),
                  pl.BlockSpec((tt, D_MODEL), lambda i: (tb(i), 0)),
                  pl.BlockSpec((D_MODEL, 128), lambda i: (0, 0)),
                  pl.BlockSpec((128, GLA_KW), lambda i: (0, 0)),
                  pl.BlockSpec((1, GLA_KW), lambda i: (0, 0)),
                  pl.BlockSpec((1, GLA_DV), lambda i: (0, 0)),
                  pl.BlockSpec((GLA_CHUNK, GLA_CHUNK), lambda i: (0, 0)),
                  pl.BlockSpec((GLA_DK, 128), lambda i: (0, 0))],
        out_specs=pl.BlockSpec((tt, GLA_VW), lambda i: (tb(i), 0)),
        out_shape=jax.ShapeDtypeStruct((tp, GLA_VW), BF16),
        scratch_shapes=[pltpu.VMEM((GLA_HEADS, GLA_DV, GLA_DK), F32),
                        pltpu.VMEM((tt, GLA_KW), F32),
                        pltpu.VMEM((tt, GLA_KW), F32),
                        pltpu.VMEM((tt, GLA_KW), F32),
                        pltpu.VMEM((tt, GLA_HEADS * 128), F32),
                        pltpu.VMEM((8, GLA_KW), F32)],
        compiler_params=_params(("arbitrary",)),
        name="gla",
    )(ug, ug, ug, ug, hn, w_gd, gk_up, gk_b, norm_w, tril, ones)


def _block_diag_groups(w):
    nb = LRU_GROUP // LRU_BLOCK
    w4 = w.reshape(-1, nb, LRU_BLOCK, LRU_BLOCK)
    out = jnp.einsum("gaij,ab->gaibj", w4, jnp.eye(nb, dtype=w.dtype))
    return out.reshape(-1, LRU_GROUP, LRU_GROUP)


def _row(v):
    return v.reshape(1, -1).astype(F32)


def _ffn(h, hn, layer, w_gate, w_val, conv_w, conv_b, w_down, gpost, gnext):
    cw = jnp.zeros((8, D_FF), F32).at[:FFN_CONV].set(conv_w[layer])
    act = _ffn_up(hn, w_gate, w_val, layer, cw, _row(conv_b[layer]))
    return _mm_res([act], [w_down[layer].astype(BF16)], h, gpost, gnext, (256,), f"ffn_down{layer}")


def kernel(x, meta_tokens, mix_pre_norm, mix_post_norm, ffn_pre_norm, ffn_post_norm, ab_w_in, lru_conv_w, lru_conv_b, lru_gate_a_w, lru_gate_a_b, lru_gate_x_w, lru_gate_x_b, lru_lambda, rwkv_shift_mu, rwkv_w0, rwkv_w2, rwkv_a0, rwkv_a2, rwkv_g2, rwkv_k_k, rwkv_k_a, rwkv_r_k, rwkv_ln_w, rwkv_ln_b, ab_w_out, gla_w_in, gla_gk_up, gla_gk_b, gla_norm_w, gla_w_out, ffn_w_gate, ffn_w_val, ffn_conv_w, ffn_conv_b, ffn_w_down):
    batch, seq, _ = x.shape
    assert batch == 1 and seq % HEAD == 0
    W = RWKV_WIDTH
    head = jnp.concatenate([jnp.zeros((PAD, D_MODEL), F32), meta_tokens.astype(F32)], axis=0)
    h, hn = _embed(head, x[0], _row(mix_pre_norm[0]))

    u = _mm(hn, ab_w_in, ab_w_in.shape[2], 768, "ab_in")
    cw = jnp.zeros((8, LRU_WIDTH), F32).at[:LRU_CONV].set(lru_conv_w[0])
    lru_out = _lru(u, cw, _row(lru_conv_b[0]),
                   _block_diag_groups(lru_gate_a_w[0]).astype(BF16), _row(lru_gate_a_b[0]),
                   _block_diag_groups(lru_gate_x_w[0]).astype(BF16), _row(lru_gate_x_b[0]),
                   _row(lru_lambda[0]))
    half = RWKV_LORA // 2
    wwa = jnp.zeros((RWKV_LORA, 2 * W), F32)
    wwa = wwa.at[:half, :W].set(rwkv_w2[0]).at[half:, W:].set(rwkv_a2[0]).astype(BF16)
    seg = jnp.kron(jnp.eye(RWKV_GROUP // RWKV_HEAD_DIM, dtype=F32),
                   jnp.ones((RWKV_HEAD_DIM, RWKV_HEAD_DIM), F32)).astype(BF16)
    mu = jnp.zeros((8, W), F32).at[:3].set(rwkv_shift_mu[0, :3 * W].reshape(3, W))
    chunk_tril = jnp.kron(jnp.eye(HEAD // RWKV_CHUNK, dtype=F32),
                          jnp.tril(jnp.ones((RWKV_CHUNK, RWKV_CHUNK), F32))).astype(BF16)
    rw_out = _rwkv(u, mu, _row(rwkv_shift_mu[0, 3 * W:]), wwa, _row(rwkv_w0[0]), _row(rwkv_a0[0]),
                   rwkv_g2[0].astype(BF16), _row(rwkv_k_k[0]), _row(rwkv_k_a[0]), _row(rwkv_r_k[0]),
                   _row(rwkv_ln_w[0]), _row(rwkv_ln_b[0]), seg, chunk_tril)
    w_out = ab_w_out[0].astype(BF16)
    h, hn = _mm_res([lru_out, rw_out], [w_out[:LRU_WIDTH], w_out[LRU_WIDTH:]], h,
                    _row(mix_post_norm[0]), _row(ffn_pre_norm[0]), (416, 256), "ab_out")
    h, hn = _ffn(h, hn, 0, ffn_w_gate, ffn_w_val, ffn_conv_w, ffn_conv_b, ffn_w_down,
                 _row(ffn_post_norm[0]), _row(mix_pre_norm[1]))

    ug = _mm(hn, gla_w_in, GLA_MAIN, 768, "gla_in")
    w_gd = jnp.zeros((D_MODEL, 128), F32).at[:, :GLA_GATE_RANK].set(gla_w_in[0, :, GLA_MAIN:]).astype(BF16)
    up = jnp.zeros((128, GLA_KW), F32).at[:GLA_GATE_RANK].set(gla_gk_up[0]).astype(BF16)
    tril = jnp.tril(jnp.ones((GLA_CHUNK, GLA_CHUNK), F32)).astype(BF16)
    ones = jnp.ones((GLA_DK, 128), BF16)
    o = _gla(ug, hn, w_gd, up, _row(gla_gk_b[0]), _row(gla_norm_w[0]), tril, ones)
    h, hn = _mm_res([o], [gla_w_out[0].astype(BF16)], h,
                    _row(mix_post_norm[1]), _row(ffn_pre_norm[1]), (416, 256), "gla_out")
    out, _ = _ffn(h, hn, 1, ffn_w_gate, ffn_w_val, ffn_conv_w, ffn_conv_b, ffn_w_down,
                  _row(ffn_post_norm[1]), None)
    return out[None]
```

```python
import functools
import math

import jax
import jax.numpy as jnp
from jax import lax
from jax.experimental import pallas as pl
from jax.experimental.pallas import tpu as pltpu

F32 = jnp.float32
BF16 = jnp.bfloat16

D_MODEL = 2048
N_META = 16
NORM_EPS = 1e-6
HEAD = 256
PAD = HEAD - N_META
LRU_WIDTH = 1024
LRU_BLOCK = 64
LRU_GROUP = 256
LRU_CONV = 4
LRU_C = 8.0
RWKV_WIDTH = 1024
RWKV_HEAD_DIM = 64
RWKV_LORA = 128
RWKV_GN_EPS = 64e-5
RWKV_CHUNK = 64
RWKV_GROUP = 128
GLA_HEADS = 4
GLA_DK = 256
GLA_DV = 512
GLA_KW = GLA_HEADS * GLA_DK
GLA_VW = GLA_HEADS * GLA_DV
GLA_GATE_RANK = 16
GLA_GATE_NORM = 16.0
GLA_CHUNK = 64
GLA_SUB = 8
GLA_BATCH = 4
GLA_MAIN = 2 * GLA_KW + 2 * GLA_VW
D_FF = 5632
FFN_CONV = 3
LOG2E = 1.4426950408889634
VMEM_LIMIT = 56 * 1024 * 1024


def _pick(n, candidates):
    for c in candidates:
        if n % c == 0:
            return c
    raise ValueError(f"no tile for {n} in {candidates}")


def _params(sem):
    return pltpu.CompilerParams(dimension_semantics=sem, vmem_limit_bytes=VMEM_LIMIT)


def _dot(a, b):
    return jnp.dot(a.astype(BF16), b.astype(BF16), preferred_element_type=F32)


def _dot_nt(a, b):
    return lax.dot_general(a.astype(BF16), b.astype(BF16), (((1,), (1,)), ((), ())),
                           preferred_element_type=F32)


def _dot_tn(a, b):
    return lax.dot_general(a.astype(BF16), b.astype(BF16), (((0,), (0,)), ((), ())),
                           preferred_element_type=F32)


def _dot_split(a, b_bf16):
    hi = a.astype(BF16)
    lo = (a - hi.astype(F32)).astype(BF16)
    return (jnp.dot(hi, b_bf16, preferred_element_type=F32)
            + jnp.dot(lo, b_bf16, preferred_element_type=F32))


def _softplus(x):
    return jnp.maximum(x, 0.0) + jnp.log1p(jnp.exp(-jnp.abs(x)))


def _expm1(x):
    u = jnp.exp(x)
    d = u - 1.0
    return jnp.where(d == 0.0, x, d * x / jnp.log(jnp.where(d == 0.0, 2.0, u)))


def _rms(x):
    return x * lax.rsqrt(jnp.mean(x * x, axis=-1, keepdims=True) + NORM_EPS)


def _row_ids(shape, base):
    return lax.broadcasted_iota(jnp.int32, shape, 0) + base


def _shift_rows(x, s, prev8):
    rolled = pltpu.roll(x, s, 0)
    rows = lax.broadcasted_iota(jnp.int32, prev8.shape, 0)
    top = jnp.where(rows < s, pltpu.roll(prev8, s, 0), rolled[0:8])
    return jnp.concatenate([top, rolled[8:]], axis=0)


def _time_block(nb):
    return lambda i: ((i + nb - 1) % nb)


def _embed_kernel(head_ref, x_ref, g_ref, hn_ref):
    i = pl.program_id(0)
    xb = jnp.where(i == pl.num_programs(0) - 1, head_ref[...], x_ref[...])
    hn_ref[...] = (_rms(xb) * g_ref[...]).astype(BF16)


def _embed(head, x2d, gain):
    seq = x2d.shape[0]
    tp = seq + HEAD
    nb = tp // HEAD
    blk = pl.BlockSpec((HEAD, D_MODEL), lambda i: (i, 0))
    return pl.pallas_call(
        _embed_kernel,
        grid=(nb,),
        in_specs=[pl.BlockSpec((HEAD, D_MODEL), lambda i: (0, 0)),
                  pl.BlockSpec((HEAD, D_MODEL), lambda i: (jnp.minimum(i, nb - 2), 0)),
                  pl.BlockSpec((1, D_MODEL), lambda i: (0, 0))],
        out_specs=blk,
        out_shape=jax.ShapeDtypeStruct((tp, D_MODEL), BF16),
        compiler_params=_params(("arbitrary",)),
        name="embed",
    )(head, x2d, gain)


def _mm_kernel(x_ref, w_ref, o_ref):
    o_ref[...] = jnp.dot(x_ref[...], w_ref[...].astype(BF16), preferred_element_type=F32)


def _mm_nt_kernel(x_ref, w_ref, o_ref):
    o_ref[...] = lax.dot_general(x_ref[...], w_ref[...].astype(BF16), (((1,), (1,)), ((), ())),
                                 preferred_element_type=F32)


def _mm(x, w3, n, tn, name, transposed=False):
    tp, k = x.shape
    tm = _pick(tp, (1664, 768, 256))
    if transposed:
        wspec = pl.BlockSpec((None, tn, k), lambda i, j: (0, j, 0))
    else:
        wspec = pl.BlockSpec((None, k, tn), lambda i, j: (0, 0, j))
    return pl.pallas_call(
        _mm_nt_kernel if transposed else _mm_kernel,
        grid=(tp // tm, n // tn),
        in_specs=[pl.BlockSpec((tm, k), lambda i, j: (i, 0)), wspec],
        out_specs=pl.BlockSpec((tm, tn), lambda i, j: (i, j)),
        out_shape=jax.ShapeDtypeStruct((tp, n), F32),
        compiler_params=_params(("arbitrary", "arbitrary")),
        name=name,
    )(x, w3)


def _mm_res_kernel(*refs, n_x, tm, tn, seq, has_next, split_h):
    x_refs = refs[:n_x]
    w_refs = refs[n_x:2 * n_x]
    n_h = 2 if split_h else 1
    h_refs = refs[2 * n_x:2 * n_x + n_h]
    gpost_ref = refs[2 * n_x + n_h]
    rest = refs[2 * n_x + n_h + 1:]
    if has_next:
        gnext_ref, ho_ref, hn_ref, acc_a, acc_b = rest
    else:
        ho_ref, acc_a, acc_b = rest
    s = pl.program_id(0)
    nt = D_MODEL // tn

    @pl.when(s == 0)
    def _():
        acc_b[...] = jnp.zeros_like(acc_b)

    def residual(cs):
        if not split_h:
            return h_refs[0][:, cs]
        return jnp.where(s == pl.num_programs(0) - 1, h_refs[1][:, cs], h_refs[0][:, cs])

    def step(cur_ref, prev_ref):
        def matmul_cols(t):
            cs = slice(t * tn, (t + 1) * tn)
            part = jnp.dot(x_refs[0][...], w_refs[0][:, cs], preferred_element_type=F32)
            for xr, wr in zip(x_refs[1:], w_refs[1:]):
                part = part + jnp.dot(xr[...], wr[:, cs], preferred_element_type=F32)
            cur_ref[t] = part

        matmul_cols(0)
        ssq = jnp.zeros((tm, 1), F32)
        for t in range(nt):
            m = prev_ref[t]
            ssq = ssq + jnp.sum(m * m, axis=-1, keepdims=True)
        scale = lax.rsqrt(ssq * (1.0 / D_MODEL) + NORM_EPS)
        rows = _row_ids((tm, 1), (s - 1) * tm)
        scale = jnp.where((rows < seq) | (rows >= seq + PAD), scale, 0.0)
        matmul_cols(1)
        ssq2 = jnp.zeros((tm, 1), F32)
        for t in range(nt):
            cs = slice(t * tn, (t + 1) * tn)
            hnew = residual(cs) + prev_ref[t] * scale * gpost_ref[:, cs]
            ho_ref[:, cs] = hnew
            if has_next:
                ssq2 = ssq2 + jnp.sum(hnew * hnew, axis=-1, keepdims=True)
        matmul_cols(2)
        if has_next:
            scale2 = lax.rsqrt(ssq2 * (1.0 / D_MODEL) + NORM_EPS)
            for t in range(nt):
                cs = slice(t * tn, (t + 1) * tn)
                hn_ref[:, cs] = (ho_ref[:, cs] * scale2 * gnext_ref[:, cs]).astype(BF16)
        for t in range(3, nt):
            matmul_cols(t)

    @pl.when(s % 2 == 0)
    def _():
        step(acc_a, acc_b)

    @pl.when(s % 2 == 1)
    def _():
        step(acc_b, acc_a)


def _mm_res(xs, ws, h, gpost, gnext, tm_candidates, name):
    split_h = isinstance(h, tuple)
    tp = xs[0].shape[0]
    seq = tp - HEAD
    kdim = xs[0].shape[1]
    has_next = gnext is not None
    rows_out = tp if has_next else seq
    tm = _pick(rows_out, tm_candidates)
    tn = 512
    n = rows_out // tm
    row = pl.BlockSpec((tm, D_MODEL), lambda s: (jnp.maximum(s - 1, 0), 0))
    gain = pl.BlockSpec((1, D_MODEL), lambda s: (0, 0))
    if split_h:
        assert tm == HEAD and has_next
        h_args = list(h)
        h_specs = [pl.BlockSpec((tm, D_MODEL), lambda s: (jnp.clip(s - 1, 0, n - 2), 0)),
                   pl.BlockSpec((HEAD, D_MODEL), lambda s: (0, 0))]
    else:
        h_args, h_specs = [h], [row]
    in_specs = ([pl.BlockSpec((tm, kdim), lambda s: (jnp.minimum(s, n - 1), 0)) for _ in xs]
                + [pl.BlockSpec((kdim, D_MODEL), lambda s: (0, 0), pipeline_mode=pl.Buffered(1)) for _ in ws]
                + h_specs + [gain] + ([gain] if has_next else []))
    out_specs = [row] + ([row] if has_next else [])
    out_shape = [jax.ShapeDtypeStruct((rows_out, D_MODEL), F32)]
    if has_next:
        out_shape.append(jax.ShapeDtypeStruct((tp, D_MODEL), BF16))
    args = list(xs) + list(ws) + h_args + [gpost] + ([gnext] if has_next else [])
    out = pl.pallas_call(
        functools.partial(_mm_res_kernel, n_x=len(xs), tm=tm, tn=tn, seq=seq, has_next=has_next,
                          split_h=split_h),
        grid=(n + 1,),
        in_specs=in_specs,
        out_specs=out_specs,
        out_shape=out_shape,
        scratch_shapes=[pltpu.VMEM((D_MODEL // tn, tm, tn), F32)] * 2,
        compiler_params=_params(("arbitrary",)),
        name=name,
    )(*args)
    return out if has_next else (out[0], None)


def _ffn_up_kernel(x_ref, xh_ref, wg_ref, wv_ref, cw_ref, cb_ref, o_ref):
    x = x_ref[...]
    gate_ext = jnp.dot(jnp.concatenate([xh_ref[...], x], axis=0), wg_ref[...].astype(BF16),
                       preferred_element_type=F32)
    gate = gate_ext[16:]
    halo = gate_ext[8:16]
    val = jnp.dot(x, wv_ref[...].astype(BF16), preferred_element_type=F32)
    cw = cw_ref[...]
    conv = (cw[2:3] * gate + cw[1:2] * _shift_rows(gate, 1, halo)
            + cw[0:1] * _shift_rows(gate, 2, halo) + cb_ref[...])
    o_ref[...] = (jax.nn.gelu(conv) * val).astype(BF16)


def _ffn_up(hn, w_gate, w_val, layer, conv_w, conv_b):
    tp = hn.shape[0]
    tm = _pick(tp, (1280, 768, 256))
    tn = 512
    halo_blocks = tm // 16
    nhalo = tp // 16
    wspec = pl.BlockSpec((None, D_MODEL, tn), lambda i, j: (layer, 0, j))
    return pl.pallas_call(
        _ffn_up_kernel,
        grid=(tp // tm, D_FF // tn),
        in_specs=[pl.BlockSpec((tm, D_MODEL), lambda i, j: (i, 0)),
                  pl.BlockSpec((16, D_MODEL), lambda i, j: ((i * halo_blocks + nhalo - 1) % nhalo, 0)),
                  wspec, wspec,
                  pl.BlockSpec((8, tn), lambda i, j: (0, j)),
                  pl.BlockSpec((1, tn), lambda i, j: (0, j))],
        out_specs=pl.BlockSpec((tm, tn), lambda i, j: (i, j)),
        out_shape=jax.ShapeDtypeStruct((tp, D_FF), BF16),
        compiler_params=_params(("arbitrary", "arbitrary")),
        name="ffn_up",
    )(hn, hn, w_gate, w_val, conv_w, conv_b)


def _lru_kernel(lx_ref, ly_ref, cw_ref, cb_ref, wa_ref, ba_ref, wx_ref, bx_ref, lam_ref,
                o_ref, halo_ref, state_ref, *, tt):
    i = pl.program_id(0)

    @pl.when(i == 0)
    def _():
        halo_ref[...] = jnp.zeros_like(halo_ref)
        state_ref[...] = jnp.zeros_like(state_ref)

    lx = lx_ref[...]
    prev = halo_ref[...]
    cw = cw_ref[...]
    xc = (cw[3:4] * lx + cw[2:3] * _shift_rows(lx, 1, prev) + cw[1:2] * _shift_rows(lx, 2, prev)
          + cw[0:1] * _shift_rows(lx, 3, prev) + cb_ref[...])
    halo_ref[...] = lx[tt - 8:tt]

    xb = xc.astype(BF16)
    ga, gx = [], []
    for g in range(LRU_WIDTH // LRU_GROUP):
        blk = xb[:, g * LRU_GROUP:(g + 1) * LRU_GROUP]
        ga.append(jnp.dot(blk, wa_ref[g], preferred_element_type=F32))
        gx.append(jnp.dot(blk, wx_ref[g], preferred_element_type=F32))
    r = jax.nn.sigmoid(jnp.concatenate(ga, axis=1) + ba_ref[...])
    gi = jax.nn.sigmoid(jnp.concatenate(gx, axis=1) + bx_ref[...])
    log_a = (-LRU_C) * r * _softplus(-lam_ref[...])
    a = jnp.exp(log_a)
    rows = _row_ids(a.shape, i * tt)
    b = jnp.where(rows >= PAD, jnp.sqrt(-_expm1(2.0 * log_a)) * (gi * xc), 0.0)

    local = lax.broadcasted_iota(jnp.int32, a.shape, 0)
    d = 1
    while d < tt:
        if d < 8:
            a_sh = jnp.where(local < d, 1.0, pltpu.roll(a, d, 0))
            b_sh = jnp.where(local < d, 0.0, pltpu.roll(b, d, 0))
        else:
            a_sh = jnp.concatenate([jnp.ones((d, LRU_WIDTH), F32), a[0:tt - d]], axis=0)
            b_sh = jnp.concatenate([jnp.zeros((d, LRU_WIDTH), F32), b[0:tt - d]], axis=0)
        b = b + a * b_sh
        a = a * a_sh
        d *= 2
    h = b + a * state_ref[...]
    state_ref[...] = h[tt - 1:tt]
    o_ref[...] = (h * jax.nn.gelu(ly_ref[...])).astype(BF16)


def _lru(u, conv_w, conv_b, wa, ba, wx, bx, lam):
    tp = u.shape[0]
    tt = HEAD
    tb = _time_block(tp // tt)
    vec = pl.BlockSpec((1, LRU_WIDTH), lambda i: (0, 0))
    wspec = pl.BlockSpec((LRU_WIDTH // LRU_GROUP, LRU_GROUP, LRU_GROUP), lambda i: (0, 0, 0))
    return pl.pallas_call(
        functools.partial(_lru_kernel, tt=tt),
        grid=(tp // tt,),
        in_specs=[pl.BlockSpec((tt, LRU_WIDTH), lambda i: (tb(i), 0)),
                  pl.BlockSpec((tt, LRU_WIDTH), lambda i: (tb(i), 1)),
                  pl.BlockSpec((8, LRU_WIDTH), lambda i: (0, 0)), vec,
                  wspec, vec, wspec, vec, vec],
        out_specs=pl.BlockSpec((tt, LRU_WIDTH), lambda i: (tb(i), 0)),
        out_shape=jax.ShapeDtypeStruct((tp, LRU_WIDTH), BF16),
        scratch_shapes=[pltpu.VMEM((8, LRU_WIDTH), F32), pltpu.VMEM((1, LRU_WIDTH), F32)],
        compiler_params=_params(("arbitrary",)),
        name="rglru",
    )(u, u, conv_w, conv_b, wa, ba, wx, bx, lam)


def _rwkv_kernel(r_ref, k_ref, vin_ref, lo_ref, mu_ref, mul_ref, wwa_ref, w0_ref, a0_ref, g2_ref,
                 kk_ref, ka_ref, rk_ref, lnw_ref, lnb_ref, seg_ref, tril_ref, o_ref,
                 last_ref, lastl_ref, state_ref, rt_ref, kt_ref, kb_ref, pb_ref, v_ref, gam_ref, y_ref,
                 x_ref, bv_ref, qkv_ref, qp_ref, *, tt):
    i = pl.program_id(0)
    C = RWKV_CHUNK
    G = RWKV_GROUP
    W = RWKV_WIDTH
    NG = W // G

    @pl.when(i == 0)
    def _():
        last_ref[...] = jnp.zeros_like(last_ref)
        lastl_ref[...] = jnp.zeros_like(lastl_ref)
        state_ref[...] = jnp.zeros_like(state_ref)

    def token_shift(x, mu, last):
        return x + (_shift_rows(x, 1, jnp.tile(last, (8, 1))) - x) * mu

    r_in, k_in, v_in, lo_in = r_ref[...], k_ref[...], vin_ref[...], lo_ref[...]
    r = token_shift(r_in, mu_ref[0:1], last_ref[0:1])
    k = token_shift(k_in, mu_ref[1:2], last_ref[1:2])
    v = token_shift(v_in, mu_ref[2:3], last_ref[2:3])
    lo = token_shift(lo_in, mul_ref[...], lastl_ref[...])
    last_ref[0:1] = r_in[tt - 1:tt]
    last_ref[1:2] = k_in[tt - 1:tt]
    last_ref[2:3] = v_in[tt - 1:tt]
    lastl_ref[...] = lo_in[tt - 1:tt]

    lora = lo[:, 0:RWKV_LORA]
    lane = lax.broadcasted_iota(jnp.int32, lora.shape, 1)
    lora = jnp.where(lane < RWKV_LORA // 2, jnp.tanh(lora), lora)
    wa = _dot(lora, wwa_ref[...])
    ell = (-math.exp(-0.5) * LOG2E) * jax.nn.sigmoid(w0_ref[...] + wa[:, 0:W])
    a = jax.nn.sigmoid(a0_ref[...] + wa[:, W:2 * W])
    g = _dot(jax.nn.sigmoid(lo[:, RWKV_LORA:]), g2_ref[...])

    seg = seg_ref[...]

    def head_sums(x):
        return jnp.concatenate([_dot(x[:, t * G:(t + 1) * G], seg) for t in range(NG)], axis=1)

    kk = k * kk_ref[...]
    kk = kk / jnp.maximum(jnp.sqrt(head_sums(kk * kk)), 1e-12)
    rows = _row_ids(k.shape, i * tt)
    k = jnp.where(rows >= PAD, k * (1.0 + (a - 1.0) * ka_ref[...]), 0.0)
    bonus = head_sums(r * k * rk_ref[...]) * v

    b = _cumsum_rows(ell, tril_ref[...])
    eb = jnp.exp2(-b)
    gam = jnp.exp2(b)
    rt_ref[...] = r * gam
    kt_ref[...] = kk * jnp.exp2(b - ell)
    kb_ref[...] = k * eb
    pb_ref[...] = kk * a * eb
    v_ref[...] = v
    gam_ref[...] = gam

    crow = lax.broadcasted_iota(jnp.int32, (C, G), 0)
    ccol = lax.broadcasted_iota(jnp.int32, (C, G), 1) % C
    strict = ccol < crow
    incl = ccol <= crow
    eye = jnp.where(ccol == crow, 1.0, 0.0)
    groups = range(NG)

    def bd(x):
        xb = x.astype(BF16)
        return jnp.concatenate([xb, xb], axis=0) * seg

    chunks = range(tt // C)

    blocks = [(slice(c * C, (c + 1) * C), slice(gi * G, (gi + 1) * G)) for c in chunks for gi in groups]

    keys = [jnp.concatenate([bd(pb_ref[rs, ls]), bd(kb_ref[rs, ls])], axis=0) for rs, ls in blocks]
    sc = [_dot_nt(jnp.concatenate([kt_ref[rs, ls], rt_ref[rs, ls]], axis=0), keys[n])
          for n, (rs, ls) in enumerate(blocks)]
    A = [jnp.where(strict, m[0:C, 0:G], 0.0) for m in sc]
    for n, (rs, ls) in enumerate(blocks):
        both = jnp.concatenate([jnp.where(strict, sc[n][0:C, G:2 * G], 0.0),
                                jnp.where(incl, sc[n][C:2 * C, G:2 * G], 0.0)], axis=0)
        prod = _dot(both, bd(v_ref[rs, ls]))
        bv_ref[rs, ls] = prod[0:C]
        qkv_ref[rs, ls] = prod[C:2 * C]
        qp_ref[rs, ls] = jnp.where(incl, sc[n][C:2 * C, 0:G], 0.0)
    X = [eye - m for m in A]
    P = [_dot(m, bd(m)) for m in A]
    n_pow = 2
    while n_pow < C:
        n_pow *= 2
        if n_pow < C:
            xp = [_dot(jnp.concatenate([x, p], axis=0), bd(p)) for x, p in zip(X, P)]
            X = [x + m[0:C] for x, m in zip(X, xp)]
            P = [m[C:2 * C] for m in xp]
        else:
            X = [x + _dot(x, bd(p)) for x, p in zip(X, P)]
    for n, (rs, ls) in enumerate(blocks):
        x_ref[rs, ls] = X[n]

    S = [state_ref[gi] for gi in groups]
    for c in chunks:
        rs = slice(c * C, (c + 1) * C)
        cols = [slice(gi * G, (gi + 1) * G) for gi in groups]
        krs = [_dot_nt(jnp.concatenate([kt_ref[rs, ls], rt_ref[rs, ls]], axis=0), S[gi])
               for gi, ls in enumerate(cols)]
        rts = [m[C:2 * C] for m in krs]
        U = [_dot(x_ref[rs, ls], bd(krs[gi][0:C] + bv_ref[rs, ls])) for gi, ls in enumerate(cols)]
        gg = [gam_ref[(c + 1) * C - 1:(c + 1) * C, ls] for ls in cols]
        upd = [_dot_tn(jnp.concatenate([v_ref[rs, ls], U[gi]], axis=0),
                       jnp.concatenate([kb_ref[rs, ls] * gg[gi], -(pb_ref[rs, ls] * gg[gi])], axis=0))
               for gi, ls in enumerate(cols)]
        for gi, ls in enumerate(cols):
            y_ref[rs, ls] = rts[gi] + qkv_ref[rs, ls] - _dot(qp_ref[rs, ls], bd(U[gi]))
        S = [S[gi] * gg[gi] + upd[gi] * seg for gi in groups]
    for gi in groups:
        state_ref[gi] = S[gi]

    y = y_ref[...]
    inv_n = 1.0 / RWKV_HEAD_DIM
    mu = head_sums(y) * inv_n
    dlt = y - mu
    var = head_sums(dlt * dlt) * inv_n
    yn = dlt * lax.rsqrt(var + RWKV_GN_EPS) * lnw_ref[...] + lnb_ref[...]
    o_ref[...] = ((yn + bonus) * g).astype(BF16)


def _rwkv(u, mu, mu_lora, wwa, w0, a0, g2, k_k, k_a, r_k, ln_w, ln_b, seg, tril):
    tp = u.shape[0]
    tt = HEAD
    tb = _time_block(tp // tt)
    W = RWKV_WIDTH
    G = RWKV_GROUP
    L2 = 2 * RWKV_LORA
    vec = pl.BlockSpec((1, W), lambda i: (0, 0))
    first = 2 * LRU_WIDTH // W
    big = pltpu.VMEM((tt, W), F32)
    return pl.pallas_call(
        functools.partial(_rwkv_kernel, tt=tt),
        grid=(tp // tt,),
        in_specs=[pl.BlockSpec((tt, W), lambda i: (tb(i), first)),
                  pl.BlockSpec((tt, W), lambda i: (tb(i), first + 1)),
                  pl.BlockSpec((tt, W), lambda i: (tb(i), first + 2)),
                  pl.BlockSpec((tt, L2), lambda i: (tb(i), (first + 3) * W // L2)),
                  pl.BlockSpec((8, W), lambda i: (0, 0)),
                  pl.BlockSpec((1, L2), lambda i: (0, 0)),
                  pl.BlockSpec((RWKV_LORA, 2 * W), lambda i: (0, 0)), vec, vec,
                  pl.BlockSpec((RWKV_LORA, W), lambda i: (0, 0)), vec, vec, vec, vec, vec,
                  pl.BlockSpec((G, G), lambda i: (0, 0)),
                  pl.BlockSpec((tt, tt), lambda i: (0, 0))],
        out_specs=pl.BlockSpec((tt, W), lambda i: (tb(i), 0)),
        out_shape=jax.ShapeDtypeStruct((tp, W), BF16),
        scratch_shapes=[pltpu.VMEM((8, W), F32), pltpu.VMEM((1, L2), F32),
                        pltpu.VMEM((W // G, G, G), F32)] + [big] * 11,
        compiler_params=_params(("arbitrary",)),
        name="rwkv7",
    )(u, u, u, u, mu, mu_lora, wwa, w0, a0, g2, k_k, k_a, r_k, ln_w, ln_b, seg, tril)


def _cumsum_rows(x, tril_bf16):
    hi = x.astype(BF16)
    lo = (x - hi.astype(F32)).astype(BF16)
    return (jnp.dot(tril_bf16, hi, preferred_element_type=F32)
            + jnp.dot(tril_bf16, lo, preferred_element_type=F32))


def _gla_kernel(q_ref, k_ref, v_ref, g_ref, hn_ref, wgd_ref, up_ref, gb_ref, nw_ref, tril_ref, ones_ref,
                o_ref, state_ref, gk_ref, qt_ref, kt_ref, a_ref, dec_ref, *, tt):
    i = pl.program_id(0)
    C = GLA_CHUNK
    SB = GLA_SUB
    DK = GLA_DK
    DV = GLA_DV
    NEG = -jnp.inf

    @pl.when(i == 0)
    def _():
        state_ref[...] = jnp.zeros_like(state_ref)

    gd = jnp.dot(hn_ref[...], wgd_ref[...], preferred_element_type=F32)
    z = _dot(gd, up_ref[...]) + gb_ref[...]
    gk_ref[...] = -_softplus(-z) * (LOG2E / GLA_GATE_NORM)

    crow = lax.broadcasted_iota(jnp.int32, (C, DK), 0)
    sub_row = crow % SB
    lane128 = lax.broadcasted_iota(jnp.int32, (C, 128), 1)
    row128 = lax.broadcasted_iota(jnp.int32, (C, 128), 0)
    diag_mask = (lane128 // SB == row128 // SB) & (lane128 < C)
    scale = DK ** -0.5

    heads = range(GLA_HEADS)
    chunks = range(tt // C)
    blocks = [(c, hd) for c in chunks for hd in heads]
    nsub = C // SB

    def row_sel(x, jj):
        return jnp.concatenate(
            [jnp.broadcast_to(x[s * SB + jj:s * SB + jj + 1], (SB, DK)) for s in range(nsub)], axis=0)

    for start in range(0, len(blocks), GLA_BATCH):
        batch = blocks[start:start + GLA_BATCH]
        idx = [(slice(c * C, (c + 1) * C), slice(hd * DK, (hd + 1) * DK)) for c, hd in batch]
        b = [_cumsum_rows(gk_ref[rs, ks], tril_ref[...]) for rs, ks in idx]
        q = [q_ref[rs, ks] * scale for rs, ks in idx]
        k = [jnp.where((crow + (i * tt + c * C)) >= PAD, k_ref[rs, ks], 0.0)
             for (c, _), (rs, ks) in zip(batch, idx)]
        for n, ((c, _), (rs, ks)) in enumerate(zip(batch, idx)):
            b_last = b[n][C - 1:C]
            qt_ref[rs, ks] = q[n] * jnp.exp2(b[n])
            kt_ref[rs, ks] = k[n] * jnp.exp2(b_last - b[n])
            dec_ref[c:c + 1, ks] = jnp.exp2(b_last)

        pieces = [[jnp.zeros((SB, 128), F32)] for _ in batch]
        for sb in range(1, nsub):
            r0 = sb * SB
            for n in range(len(batch)):
                ref_b = b[n][r0 - 1:r0]
                qh = q[n][r0:r0 + SB] * jnp.exp2(b[n][r0:r0 + SB] - ref_b)
                kh = k[n][0:r0] * jnp.exp2(ref_b - b[n][0:r0])
                kh = jnp.concatenate([kh, jnp.zeros((128 - r0, DK), F32)], axis=0)
                pieces[n].append(_dot_nt(qh, kh))

        diag = [jnp.zeros((C, 128), F32) for _ in batch]
        for jj in range(SB):
            for n in range(len(batch)):
                e = jnp.exp2(jnp.where(sub_row >= jj, b[n] - row_sel(b[n], jj), NEG))
                col = _dot(q[n] * e * row_sel(k[n], jj), ones_ref[...])
                diag[n] = jnp.where(lane128 % SB == jj, col, diag[n])
        for n, (c, hd) in enumerate(batch):
            a_ref[c * C:(c + 1) * C, hd * 128:(hd + 1) * 128] = (
                jnp.concatenate(pieces[n], axis=0) + jnp.where(diag_mask, diag[n], 0.0))

    for c in chunks:
        rs = slice(c * C, (c + 1) * C)
        valid_v = (lax.broadcasted_iota(jnp.int32, (C, DV), 0) + (i * tt + c * C)) >= PAD
        ksl = [slice(hd * DK, (hd + 1) * DK) for hd in heads]
        vsl = [slice(hd * DV, (hd + 1) * DV) for hd in heads]
        v = [jnp.where(valid_v, v_ref[rs, vs], 0.0) for vs in vsl]
        o = [_dot_nt(qt_ref[rs, ksl[hd]], state_ref[hd]) for hd in heads]
        o = [o[hd] + _dot(a_ref[rs, hd * 128:(hd + 1) * 128],
                          jnp.concatenate([v[hd], jnp.zeros((128 - C, DV), F32)], axis=0))
             for hd in heads]
        for hd in heads:
            on = o[hd] * lax.rsqrt(jnp.mean(o[hd] * o[hd], axis=-1, keepdims=True) + NORM_EPS) * nw_ref[...]
            gate = g_ref[rs, vsl[hd]]
            o_ref[rs, vsl[hd]] = (on * (gate * jax.nn.sigmoid(gate))).astype(BF16)
        upd = [_dot_tn(v[hd], kt_ref[rs, ksl[hd]]) for hd in heads]
        for hd in heads:
            state_ref[hd] = state_ref[hd] * dec_ref[c:c + 1, ksl[hd]] + upd[hd]


def _gla(ug, hn, w_gd, gk_up, gk_b, norm_w, tril, ones):
    tp = ug.shape[0]
    tt = HEAD
    tb = _time_block(tp // tt)
    return pl.pallas_call(
        functools.partial(_gla_kernel, tt=tt),
        grid=(tp // tt,),
        in_specs=[pl.BlockSpec((tt, GLA_KW), lambda i: (tb(i), 0)),
                  pl.BlockSpec((tt, GLA_KW), lambda i: (tb(i), 1)),
                  pl.BlockSpec((tt, GLA_VW), lambda i: (tb(i), 1)),
                  pl.BlockSpec((tt, GLA_VW), lambda i: (tb(i), 2)),
                  pl.BlockSpec((tt, D_MODEL), lambda i: (tb(i), 0)),
                  pl.BlockSpec((D_MODEL, 128), lambda i: (0, 0)),
                  pl.BlockSpec((128, GLA_KW), lambda i: (0, 0)),
                  pl.BlockSpec((1, GLA_KW), lambda i: (0, 0)),
                  pl.BlockSpec((1, GLA_DV), lambda i: (0, 0)),
                  pl.BlockSpec((GLA_CHUNK, GLA_CHUNK), lambda i: (0, 0)),
                  pl.BlockSpec((GLA_DK, 128), lambda i: (0, 0))],
        out_specs=pl.BlockSpec((tt, GLA_VW), lambda i: (tb(i), 0)),
        out_shape=jax.ShapeDtypeStruct((tp, GLA_VW), BF16),
        scratch_shapes=[pltpu.VMEM((GLA_HEADS, GLA_DV, GLA_DK), F32),
                        pltpu.VMEM((tt, GLA_KW), F32),
                        pltpu.VMEM((tt, GLA_KW), F32),
                        pltpu.VMEM((tt, GLA_KW), F32),
                        pltpu.VMEM((tt, GLA_HEADS * 128), F32),
                        pltpu.VMEM((8, GLA_KW), F32)],
        compiler_params=_params(("arbitrary",)),
        name="gla",
    )(ug, ug, ug, ug, hn, w_gd, gk_up, gk_b, norm_w, tril, ones)


def _block_diag_groups(w):
    nb = LRU_GROUP // LRU_BLOCK
    w4 = w.reshape(-1, nb, LRU_BLOCK, LRU_BLOCK)
    out = jnp.einsum("gaij,ab->gaibj", w4, jnp.eye(nb, dtype=w.dtype))
    return out.reshape(-1, LRU_GROUP, LRU_GROUP)


def _row(v):
    return v.reshape(1, -1).astype(F32)


def _ffn(h, hn, layer, w_gate, w_val, conv_w, conv_b, w_down, gpost, gnext):
    cw = jnp.zeros((8, D_FF), F32).at[:FFN_CONV].set(conv_w[layer])
    act = _ffn_up(hn, w_gate, w_val, layer, cw, _row(conv_b[layer]))
    return _mm_res([act], [w_down[layer].astype(BF16)], h, gpost, gnext, (256,), f"ffn_down{layer}")


def kernel(x, meta_tokens, mix_pre_norm, mix_post_norm, ffn_pre_norm, ffn_post_norm, ab_w_in, lru_conv_w, lru_conv_b, lru_gate_a_w, lru_gate_a_b, lru_gate_x_w, lru_gate_x_b, lru_lambda, rwkv_shift_mu, rwkv_w0, rwkv_w2, rwkv_a0, rwkv_a2, rwkv_g2, rwkv_k_k, rwkv_k_a, rwkv_r_k, rwkv_ln_w, rwkv_ln_b, ab_w_out, gla_w_in, gla_gk_up, gla_gk_b, gla_norm_w, gla_w_out, ffn_w_gate, ffn_w_val, ffn_conv_w, ffn_conv_b, ffn_w_down):
    batch, seq, _ = x.shape
    assert batch == 1 and seq % HEAD == 0
    W = RWKV_WIDTH
    head = jnp.concatenate([jnp.zeros((PAD, D_MODEL), F32), meta_tokens.astype(F32)], axis=0)
    hn = _embed(head, x[0], _row(mix_pre_norm[0]))

    u = _mm(hn, ab_w_in, ab_w_in.shape[2], 768, "ab_in")
    cw = jnp.zeros((8, LRU_WIDTH), F32).at[:LRU_CONV].set(lru_conv_w[0])
    lru_out = _lru(u, cw, _row(lru_conv_b[0]),
                   _block_diag_groups(lru_gate_a_w[0]).astype(BF16), _row(lru_gate_a_b[0]),
                   _block_diag_groups(lru_gate_x_w[0]).astype(BF16), _row(lru_gate_x_b[0]),
                   _row(lru_lambda[0]))
    half = RWKV_LORA // 2
    wwa = jnp.zeros((RWKV_LORA, 2 * W), F32)
    wwa = wwa.at[:half, :W].set(rwkv_w2[0]).at[half:, W:].set(rwkv_a2[0]).astype(BF16)
    seg = jnp.kron(jnp.eye(RWKV_GROUP // RWKV_HEAD_DIM, dtype=F32),
                   jnp.ones((RWKV_HEAD_DIM, RWKV_HEAD_DIM), F32)).astype(BF16)
    mu = jnp.zeros((8, W), F32).at[:3].set(rwkv_shift_mu[0, :3 * W].reshape(3, W))
    chunk_tril = jnp.kron(jnp.eye(HEAD // RWKV_CHUNK, dtype=F32),
                          jnp.tril(jnp.ones((RWKV_CHUNK, RWKV_CHUNK), F32))).astype(BF16)
    rw_out = _rwkv(u, mu, _row(rwkv_shift_mu[0, 3 * W:]), wwa, _row(rwkv_w0[0]), _row(rwkv_a0[0]),
                   rwkv_g2[0].astype(BF16), _row(rwkv_k_k[0]), _row(rwkv_k_a[0]), _row(rwkv_r_k[0]),
                   _row(rwkv_ln_w[0]), _row(rwkv_ln_b[0]), seg, chunk_tril)
    w_out = ab_w_out[0].astype(BF16)
    h, hn = _mm_res([lru_out, rw_out], [w_out[:LRU_WIDTH], w_out[LRU_WIDTH:]], (x[0], head),
                    _row(mix_post_norm[0]), _row(ffn_pre_norm[0]), (HEAD,), "ab_out")
    h, hn = _ffn(h, hn, 0, ffn_w_gate, ffn_w_val, ffn_conv_w, ffn_conv_b, ffn_w_down,
                 _row(ffn_post_norm[0]), _row(mix_pre_norm[1]))

    ug = _mm(hn, jnp.swapaxes(gla_w_in, 1, 2), GLA_MAIN, 768, "gla_in", transposed=True)
    w_gd = jnp.zeros((D_MODEL, 128), F32).at[:, :GLA_GATE_RANK].set(gla_w_in[0, :, GLA_MAIN:]).astype(BF16)
    up = jnp.zeros((128, GLA_KW), F32).at[:GLA_GATE_RANK].set(gla_gk_up[0]).astype(BF16)
    tril = jnp.tril(jnp.ones((GLA_CHUNK, GLA_CHUNK), F32)).astype(BF16)
    ones = jnp.ones((GLA_DK, 128), BF16)
    o = _gla(ug, hn, w_gd, up, _row(gla_gk_b[0]), _row(gla_norm_w[0]), tril, ones)
    h, hn = _mm_res([o], [gla_w_out[0].astype(BF16)], h,
                    _row(mix_post_norm[1]), _row(ffn_pre_norm[1]), (416, 256), "gla_out")
    out, _ = _ffn(h, hn, 1, ffn_w_gate, ffn_w_val, ffn_conv_w, ffn_conv_b, ffn_w_down,
                  _row(ffn_post_norm[1]), None)
    return out[None]
```

```python
import functools
import math

import jax
import jax.numpy as jnp
from jax import lax
from jax.experimental import pallas as pl
from jax.experimental.pallas import tpu as pltpu

F32 = jnp.float32
BF16 = jnp.bfloat16

D_MODEL = 2048
N_META = 16
NORM_EPS = 1e-6
HEAD = 256
PAD = HEAD - N_META
LRU_WIDTH = 1024
LRU_BLOCK = 64
LRU_GROUP = 256
LRU_CONV = 4
LRU_C = 8.0
RWKV_WIDTH = 1024
RWKV_HEAD_DIM = 64
RWKV_LORA = 128
RWKV_GN_EPS = 64e-5
RWKV_CHUNK = 64
RWKV_GROUP = 128
GLA_HEADS = 4
GLA_DK = 256
GLA_DV = 512
GLA_KW = GLA_HEADS * GLA_DK
GLA_VW = GLA_HEADS * GLA_DV
GLA_GATE_RANK = 16
GLA_GATE_NORM = 16.0
GLA_CHUNK = 64
GLA_SUB = 8
GLA_BATCH = 4
GLA_MAIN = 2 * GLA_KW + 2 * GLA_VW
D_FF = 5632
FFN_CONV = 3
LOG2E = 1.4426950408889634
VMEM_LIMIT = 56 * 1024 * 1024

MM_ROW_TILES = (1664, 768, 256)
MM_COL_TILE = 768
FFN_ROW_TILES = (1280, 768, 256)
FFN_COL_TILE = 512
RES_COL_TILE = 512
RES_ROW_TILES_MIX = (416, 256)
RES_ROW_TILES_FFN = (256,)


def _pick(n, candidates):
    for c in candidates:
        if n % c == 0:
            return c
    raise ValueError(f"no tile for {n} in {candidates}")


def _params(sem):
    return pltpu.CompilerParams(dimension_semantics=sem, vmem_limit_bytes=VMEM_LIMIT)


def _dot(a, b):
    return jnp.dot(a.astype(BF16), b.astype(BF16), preferred_element_type=F32)


def _dot_nt(a, b):
    return lax.dot_general(a.astype(BF16), b.astype(BF16), (((1,), (1,)), ((), ())),
                           preferred_element_type=F32)


def _dot_tn(a, b):
    return lax.dot_general(a.astype(BF16), b.astype(BF16), (((0,), (0,)), ((), ())),
                           preferred_element_type=F32)


def _softplus(x):
    return jnp.maximum(x, 0.0) + jnp.log1p(jnp.exp(-jnp.abs(x)))


def _one_minus_exp2x(x):
    t = jnp.tanh(x)
    return -2.0 * t / (1.0 - t)


def _rms(x):
    return x * lax.rsqrt(jnp.mean(x * x, axis=-1, keepdims=True) + NORM_EPS)


def _row_ids(shape, base):
    return lax.broadcasted_iota(jnp.int32, shape, 0) + base


def _shift_rows(x, s, prev8):
    rolled = pltpu.roll(x, s, 0)
    rows = lax.broadcasted_iota(jnp.int32, prev8.shape, 0)
    top = jnp.where(rows < s, pltpu.roll(prev8, s, 0), rolled[0:8])
    return jnp.concatenate([top, rolled[8:]], axis=0)


def _time_block(nb):
    return lambda i: ((i + nb - 1) % nb)


def _embed_kernel(head_ref, x_ref, g_ref, hn_ref):
    i = pl.program_id(0)
    xb = jnp.where(i == pl.num_programs(0) - 1, head_ref[...], x_ref[...])
    hn_ref[...] = (_rms(xb) * g_ref[...]).astype(BF16)


def _embed(head, x2d, gain):
    seq = x2d.shape[0]
    tp = seq + HEAD
    nb = tp // HEAD
    blk = pl.BlockSpec((HEAD, D_MODEL), lambda i: (i, 0))
    return pl.pallas_call(
        _embed_kernel,
        grid=(nb,),
        in_specs=[pl.BlockSpec((HEAD, D_MODEL), lambda i: (0, 0)),
                  pl.BlockSpec((HEAD, D_MODEL), lambda i: (jnp.minimum(i, nb - 2), 0)),
                  pl.BlockSpec((1, D_MODEL), lambda i: (0, 0))],
        out_specs=blk,
        out_shape=jax.ShapeDtypeStruct((tp, D_MODEL), BF16),
        compiler_params=_params(("arbitrary",)),
        name="embed",
    )(head, x2d, gain)


def _mm_kernel(x_ref, w_ref, o_ref):
    o_ref[...] = jnp.dot(x_ref[...], w_ref[...].astype(BF16), preferred_element_type=F32)


def _mm_nt_kernel(x_ref, w_ref, o_ref):
    o_ref[...] = lax.dot_general(x_ref[...], w_ref[...].astype(BF16), (((1,), (1,)), ((), ())),
                                 preferred_element_type=F32)


def _mm(x, w3, n, tn, name, transposed=False):
    tp, k = x.shape
    tm = _pick(tp, MM_ROW_TILES)
    if transposed:
        wspec = pl.BlockSpec((None, tn, k), lambda i, j: (0, j, 0))
    else:
        wspec = pl.BlockSpec((None, k, tn), lambda i, j: (0, 0, j))
    return pl.pallas_call(
        _mm_nt_kernel if transposed else _mm_kernel,
        grid=(tp // tm, n // tn),
        in_specs=[pl.BlockSpec((tm, k), lambda i, j: (i, 0)), wspec],
        out_specs=pl.BlockSpec((tm, tn), lambda i, j: (i, j)),
        out_shape=jax.ShapeDtypeStruct((tp, n), F32),
        compiler_params=_params(("arbitrary", "arbitrary")),
        name=name,
    )(x, w3)


def _mm_res_kernel(*refs, n_x, tm, tn, seq, has_next, split_h):
    x_refs = refs[:n_x]
    w_refs = refs[n_x:2 * n_x]
    n_h = 2 if split_h else 1
    h_refs = refs[2 * n_x:2 * n_x + n_h]
    gpost_ref = refs[2 * n_x + n_h]
    rest = refs[2 * n_x + n_h + 1:]
    if has_next:
        gnext_ref, ho_ref, hn_ref, acc_a, acc_b = rest
    else:
        ho_ref, acc_a, acc_b = rest
    s = pl.program_id(0)
    nt = D_MODEL // tn

    @pl.when(s == 0)
    def _():
        acc_b[...] = jnp.zeros_like(acc_b)

    def residual(cs):
        if not split_h:
            return h_refs[0][:, cs]
        return jnp.where(s == pl.num_programs(0) - 1, h_refs[1][:, cs], h_refs[0][:, cs])

    def step(cur_ref, prev_ref):
        def matmul_cols(t):
            cs = slice(t * tn, (t + 1) * tn)
            part = jnp.dot(x_refs[0][...], w_refs[0][:, cs], preferred_element_type=F32)
            for xr, wr in zip(x_refs[1:], w_refs[1:]):
                part = part + jnp.dot(xr[...], wr[:, cs], preferred_element_type=F32)
            cur_ref[t] = part

        matmul_cols(0)
        ssq = jnp.zeros((tm, 1), F32)
        for t in range(nt):
            m = prev_ref[t]
            ssq = ssq + jnp.sum(m * m, axis=-1, keepdims=True)
        scale = lax.rsqrt(ssq * (1.0 / D_MODEL) + NORM_EPS)
        rows = _row_ids((tm, 1), (s - 1) * tm)
        scale = jnp.where((rows < seq) | (rows >= seq + PAD), scale, 0.0)
        matmul_cols(1)
        ssq2 = jnp.zeros((tm, 1), F32)
        for t in range(nt):
            cs = slice(t * tn, (t + 1) * tn)
            hnew = residual(cs) + prev_ref[t] * scale * gpost_ref[:, cs]
            ho_ref[:, cs] = hnew
            if has_next:
                ssq2 = ssq2 + jnp.sum(hnew * hnew, axis=-1, keepdims=True)
        matmul_cols(2)
        if has_next:
            scale2 = lax.rsqrt(ssq2 * (1.0 / D_MODEL) + NORM_EPS)
            for t in range(nt):
                cs = slice(t * tn, (t + 1) * tn)
                hn_ref[:, cs] = (ho_ref[:, cs] * scale2 * gnext_ref[:, cs]).astype(BF16)
        for t in range(3, nt):
            matmul_cols(t)

    @pl.when(s % 2 == 0)
    def _():
        step(acc_a, acc_b)

    @pl.when(s % 2 == 1)
    def _():
        step(acc_b, acc_a)


def _mm_res(xs, ws, h, gpost, gnext, tm_candidates, name):
    split_h = isinstance(h, tuple)
    tp = xs[0].shape[0]
    seq = tp - HEAD
    kdim = xs[0].shape[1]
    has_next = gnext is not None
    rows_out = tp if has_next else seq
    tm = _pick(rows_out, tm_candidates)
    tn = RES_COL_TILE
    n = rows_out // tm
    row = pl.BlockSpec((tm, D_MODEL), lambda s: (jnp.maximum(s - 1, 0), 0))
    gain = pl.BlockSpec((1, D_MODEL), lambda s: (0, 0))
    if split_h:
        assert tm == HEAD and has_next
        h_args = list(h)
        h_specs = [pl.BlockSpec((tm, D_MODEL), lambda s: (jnp.clip(s - 1, 0, n - 2), 0)),
                   pl.BlockSpec((HEAD, D_MODEL), lambda s: (0, 0))]
    else:
        h_args, h_specs = [h], [row]
    w_args, w_specs = [], []
    for w in ws:
        if isinstance(w, tuple):
            w_args.append(w[0])
            w_specs.append(pl.BlockSpec((None, kdim, D_MODEL), lambda s, layer=w[1]: (layer, 0, 0),
                                        pipeline_mode=pl.Buffered(1)))
        else:
            w_args.append(w)
            w_specs.append(pl.BlockSpec((kdim, D_MODEL), lambda s: (0, 0), pipeline_mode=pl.Buffered(1)))
    in_specs = ([pl.BlockSpec((tm, kdim), lambda s: (jnp.minimum(s, n - 1), 0)) for _ in xs]
                + w_specs + h_specs + [gain] + ([gain] if has_next else []))
    out_specs = [row] + ([row] if has_next else [])
    out_shape = [jax.ShapeDtypeStruct((rows_out, D_MODEL), F32)]
    if has_next:
        out_shape.append(jax.ShapeDtypeStruct((tp, D_MODEL), BF16))
    args = list(xs) + w_args + h_args + [gpost] + ([gnext] if has_next else [])
    out = pl.pallas_call(
        functools.partial(_mm_res_kernel, n_x=len(xs), tm=tm, tn=tn, seq=seq, has_next=has_next,
                          split_h=split_h),
        grid=(n + 1,),
        in_specs=in_specs,
        out_specs=out_specs,
        out_shape=out_shape,
        scratch_shapes=[pltpu.VMEM((D_MODEL // tn, tm, tn), F32)] * 2,
        compiler_params=_params(("arbitrary",)),
        name=name,
    )(*args)
    return out if has_next else (out[0], None)


def _ffn_up_kernel(x_ref, xh_ref, wg_ref, wv_ref, cw_ref, cb_ref, o_ref):
    x = x_ref[...]
    gate_ext = jnp.dot(jnp.concatenate([xh_ref[...], x], axis=0), wg_ref[...].astype(BF16),
                       preferred_element_type=F32)
    gate = gate_ext[16:]
    halo = gate_ext[8:16]
    val = jnp.dot(x, wv_ref[...].astype(BF16), preferred_element_type=F32)
    cw = cw_ref[...]
    conv = (cw[2:3] * gate + cw[1:2] * _shift_rows(gate, 1, halo)
            + cw[0:1] * _shift_rows(gate, 2, halo) + cb_ref[...])
    o_ref[...] = (jax.nn.gelu(conv) * val).astype(BF16)


def _ffn_up(hn, w_gate, w_val, layer, conv_w, conv_b):
    tp = hn.shape[0]
    tm = _pick(tp, FFN_ROW_TILES)
    tn = FFN_COL_TILE
    halo_blocks = tm // 16
    nhalo = tp // 16
    wspec = pl.BlockSpec((None, D_MODEL, tn), lambda i, j: (layer, 0, j))
    return pl.pallas_call(
        _ffn_up_kernel,
        grid=(tp // tm, D_FF // tn),
        in_specs=[pl.BlockSpec((tm, D_MODEL), lambda i, j: (i, 0)),
                  pl.BlockSpec((16, D_MODEL), lambda i, j: ((i * halo_blocks + nhalo - 1) % nhalo, 0)),
                  wspec, wspec,
                  pl.BlockSpec((8, tn), lambda i, j: (0, j)),
                  pl.BlockSpec((1, tn), lambda i, j: (0, j))],
        out_specs=pl.BlockSpec((tm, tn), lambda i, j: (i, j)),
        out_shape=jax.ShapeDtypeStruct((tp, D_FF), BF16),
        compiler_params=_params(("arbitrary", "arbitrary")),
        name="ffn_up",
    )(hn, hn, w_gate, w_val, conv_w, conv_b)


def _lru_kernel(lx_ref, ly_ref, cw_ref, cb_ref, wa_ref, ba_ref, wx_ref, bx_ref, lam_ref,
                o_ref, halo_ref, state_ref, *, tt):
    i = pl.program_id(0)

    @pl.when(i == 0)
    def _():
        halo_ref[...] = jnp.zeros_like(halo_ref)
        state_ref[...] = jnp.zeros_like(state_ref)

    lx = lx_ref[...]
    prev = halo_ref[...]
    cw = cw_ref[...]
    xc = (cw[3:4] * lx + cw[2:3] * _shift_rows(lx, 1, prev) + cw[1:2] * _shift_rows(lx, 2, prev)
          + cw[0:1] * _shift_rows(lx, 3, prev) + cb_ref[...])
    halo_ref[...] = lx[tt - 8:tt]

    xb = xc.astype(BF16)
    ga, gx = [], []
    for g in range(LRU_WIDTH // LRU_GROUP):
        blk = xb[:, g * LRU_GROUP:(g + 1) * LRU_GROUP]
        ga.append(jnp.dot(blk, wa_ref[g], preferred_element_type=F32))
        gx.append(jnp.dot(blk, wx_ref[g], preferred_element_type=F32))
    r = jax.nn.sigmoid(jnp.concatenate(ga, axis=1) + ba_ref[...])
    gi = jax.nn.sigmoid(jnp.concatenate(gx, axis=1) + bx_ref[...])
    log_a = (-LRU_C) * r * _softplus(-lam_ref[...])
    a = jnp.exp(log_a)
    rows = _row_ids(a.shape, i * tt)
    b = jnp.where(rows >= PAD, jnp.sqrt(_one_minus_exp2x(log_a)) * (gi * xc), 0.0)

    local = lax.broadcasted_iota(jnp.int32, a.shape, 0)
    d = 1
    while d < tt:
        if d < 8:
            a_sh = jnp.where(local < d, 1.0, pltpu.roll(a, d, 0))
            b_sh = jnp.where(local < d, 0.0, pltpu.roll(b, d, 0))
        else:
            a_sh = jnp.concatenate([jnp.ones((d, LRU_WIDTH), F32), a[0:tt - d]], axis=0)
            b_sh = jnp.concatenate([jnp.zeros((d, LRU_WIDTH), F32), b[0:tt - d]], axis=0)
        b = b + a * b_sh
        a = a * a_sh
        d *= 2
    h = b + a * state_ref[...]
    state_ref[...] = h[tt - 1:tt]
    o_ref[...] = (h * jax.nn.gelu(ly_ref[...])).astype(BF16)


def _lru(u, conv_w, conv_b, wa, ba, wx, bx, lam):
    tp = u.shape[0]
    tt = HEAD
    tb = _time_block(tp // tt)
    vec = pl.BlockSpec((1, LRU_WIDTH), lambda i: (0, 0))
    wspec = pl.BlockSpec((LRU_WIDTH // LRU_GROUP, LRU_GROUP, LRU_GROUP), lambda i: (0, 0, 0))
    return pl.pallas_call(
        functools.partial(_lru_kernel, tt=tt),
        grid=(tp // tt,),
        in_specs=[pl.BlockSpec((tt, LRU_WIDTH), lambda i: (tb(i), 0)),
                  pl.BlockSpec((tt, LRU_WIDTH), lambda i: (tb(i), 1)),
                  pl.BlockSpec((8, LRU_WIDTH), lambda i: (0, 0)), vec,
                  wspec, vec, wspec, vec, vec],
        out_specs=pl.BlockSpec((tt, LRU_WIDTH), lambda i: (tb(i), 0)),
        out_shape=jax.ShapeDtypeStruct((tp, LRU_WIDTH), BF16),
        scratch_shapes=[pltpu.VMEM((8, LRU_WIDTH), F32), pltpu.VMEM((1, LRU_WIDTH), F32)],
        compiler_params=_params(("arbitrary",)),
        name="rglru",
    )(u, u, conv_w, conv_b, wa, ba, wx, bx, lam)


def _rwkv_kernel(r_ref, k_ref, vin_ref, lo_ref, mu_ref, mul_ref, wwa_ref, w0_ref, a0_ref, g2_ref,
                 kk_ref, ka_ref, rk_ref, lnw_ref, lnb_ref, seg_ref, tril_ref, o_ref,
                 last_ref, lastl_ref, state_ref, rt_ref, kt_ref, kb_ref, pb_ref, v_ref, gam_ref, y_ref,
                 x_ref, bv_ref, qkv_ref, qp_ref, *, tt):
    i = pl.program_id(0)
    C = RWKV_CHUNK
    G = RWKV_GROUP
    W = RWKV_WIDTH
    NG = W // G

    @pl.when(i == 0)
    def _():
        last_ref[...] = jnp.zeros_like(last_ref)
        lastl_ref[...] = jnp.zeros_like(lastl_ref)
        state_ref[...] = jnp.zeros_like(state_ref)

    def token_shift(x, mu, last):
        return x + (_shift_rows(x, 1, jnp.tile(last, (8, 1))) - x) * mu

    r_in, k_in, v_in, lo_in = r_ref[...], k_ref[...], vin_ref[...], lo_ref[...]
    r = token_shift(r_in, mu_ref[0:1], last_ref[0:1])
    k = token_shift(k_in, mu_ref[1:2], last_ref[1:2])
    v = token_shift(v_in, mu_ref[2:3], last_ref[2:3])
    lo = token_shift(lo_in, mul_ref[...], lastl_ref[...])
    last_ref[0:1] = r_in[tt - 1:tt]
    last_ref[1:2] = k_in[tt - 1:tt]
    last_ref[2:3] = v_in[tt - 1:tt]
    lastl_ref[...] = lo_in[tt - 1:tt]

    lora = lo[:, 0:RWKV_LORA]
    lane = lax.broadcasted_iota(jnp.int32, lora.shape, 1)
    lora = jnp.where(lane < RWKV_LORA // 2, jnp.tanh(lora), lora)
    wa = _dot(lora, wwa_ref[...])
    ell = (-math.exp(-0.5) * LOG2E) * jax.nn.sigmoid(w0_ref[...] + wa[:, 0:W])
    a = jax.nn.sigmoid(a0_ref[...] + wa[:, W:2 * W])
    g = _dot(jax.nn.sigmoid(lo[:, RWKV_LORA:]), g2_ref[...])

    seg = seg_ref[...]

    def head_sums(x):
        return jnp.concatenate([_dot(x[:, t * G:(t + 1) * G], seg) for t in range(NG)], axis=1)

    kk = k * kk_ref[...]
    kk = kk / jnp.maximum(jnp.sqrt(head_sums(kk * kk)), 1e-12)
    rows = _row_ids(k.shape, i * tt)
    k = jnp.where(rows >= PAD, k * (1.0 + (a - 1.0) * ka_ref[...]), 0.0)
    bonus = head_sums(r * k * rk_ref[...]) * v

    b = _cumsum_rows(ell, tril_ref[...])
    eb = jnp.exp2(-b)
    gam = jnp.exp2(b)
    rt_ref[...] = r * gam
    kt_ref[...] = kk * jnp.exp2(b - ell)
    kb_ref[...] = k * eb
    pb_ref[...] = kk * a * eb
    v_ref[...] = v
    gam_ref[...] = gam

    crow = lax.broadcasted_iota(jnp.int32, (C, G), 0)
    ccol = lax.broadcasted_iota(jnp.int32, (C, G), 1) % C
    strict = ccol < crow
    incl = ccol <= crow
    eye = jnp.where(ccol == crow, 1.0, 0.0)
    groups = range(NG)

    def bd(x):
        xb = x.astype(BF16)
        return jnp.concatenate([xb, xb], axis=0) * seg

    chunks = range(tt // C)

    blocks = [(slice(c * C, (c + 1) * C), slice(gi * G, (gi + 1) * G)) for c in chunks for gi in groups]

    keys = [jnp.concatenate([bd(pb_ref[rs, ls]), bd(kb_ref[rs, ls])], axis=0) for rs, ls in blocks]
    sc = [_dot_nt(jnp.concatenate([kt_ref[rs, ls], rt_ref[rs, ls]], axis=0), keys[n])
          for n, (rs, ls) in enumerate(blocks)]
    A = [jnp.where(strict, m[0:C, 0:G], 0.0) for m in sc]
    for n, (rs, ls) in enumerate(blocks):
        both = jnp.concatenate([jnp.where(strict, sc[n][0:C, G:2 * G], 0.0),
                                jnp.where(incl, sc[n][C:2 * C, G:2 * G], 0.0)], axis=0)
        prod = _dot(both, bd(v_ref[rs, ls]))
        bv_ref[rs, ls] = prod[0:C]
        qkv_ref[rs, ls] = prod[C:2 * C]
        qp_ref[rs, ls] = jnp.where(incl, sc[n][C:2 * C, 0:G], 0.0)
    X = [eye - m for m in A]
    P = [_dot(m, bd(m)) for m in A]
    n_pow = 2
    while n_pow < C:
        n_pow *= 2
        if n_pow < C:
            xp = [_dot(jnp.concatenate([x, p], axis=0), bd(p)) for x, p in zip(X, P)]
            X = [x + m[0:C] for x, m in zip(X, xp)]
            P = [m[C:2 * C] for m in xp]
        else:
            X = [x + _dot(x, bd(p)) for x, p in zip(X, P)]
    for n, (rs, ls) in enumerate(blocks):
        x_ref[rs, ls] = X[n]

    S = [state_ref[gi] for gi in groups]
    for c in chunks:
        rs = slice(c * C, (c + 1) * C)
        cols = [slice(gi * G, (gi + 1) * G) for gi in groups]
        krs = [_dot_nt(jnp.concatenate([kt_ref[rs, ls], rt_ref[rs, ls]], axis=0), S[gi])
               for gi, ls in enumerate(cols)]
        rts = [m[C:2 * C] for m in krs]
        U = [_dot(x_ref[rs, ls], bd(krs[gi][0:C] + bv_ref[rs, ls])) for gi, ls in enumerate(cols)]
        gg = [gam_ref[(c + 1) * C - 1:(c + 1) * C, ls] for ls in cols]
        upd = [_dot_tn(jnp.concatenate([v_ref[rs, ls], U[gi]], axis=0),
                       jnp.concatenate([kb_ref[rs, ls] * gg[gi], -(pb_ref[rs, ls] * gg[gi])], axis=0))
               for gi, ls in enumerate(cols)]
        for gi, ls in enumerate(cols):
            y_ref[rs, ls] = rts[gi] + qkv_ref[rs, ls] - _dot(qp_ref[rs, ls], bd(U[gi]))
        S = [S[gi] * gg[gi] + upd[gi] * seg for gi in groups]
    for gi in groups:
        state_ref[gi] = S[gi]

    y = y_ref[...]
    inv_n = 1.0 / RWKV_HEAD_DIM
    mu = head_sums(y) * inv_n
    dlt = y - mu
    var = head_sums(dlt * dlt) * inv_n
    yn = dlt * lax.rsqrt(var + RWKV_GN_EPS) * lnw_ref[...] + lnb_ref[...]
    o_ref[...] = ((yn + bonus) * g).astype(BF16)


def _rwkv(u, mu, mu_lora, wwa, w0, a0, g2, k_k, k_a, r_k, ln_w, ln_b, seg, tril):
    tp = u.shape[0]
    tt = HEAD
    tb = _time_block(tp // tt)
    W = RWKV_WIDTH
    G = RWKV_GROUP
    L2 = 2 * RWKV_LORA
    vec = pl.BlockSpec((1, W), lambda i: (0, 0))
    first = 2 * LRU_WIDTH // W
    big = pltpu.VMEM((tt, W), F32)
    return pl.pallas_call(
        functools.partial(_rwkv_kernel, tt=tt),
        grid=(tp // tt,),
        in_specs=[pl.BlockSpec((tt, W), lambda i: (tb(i), first)),
                  pl.BlockSpec((tt, W), lambda i: (tb(i), first + 1)),
                  pl.BlockSpec((tt, W), lambda i: (tb(i), first + 2)),
                  pl.BlockSpec((tt, L2), lambda i: (tb(i), (first + 3) * W // L2)),
                  pl.BlockSpec((8, W), lambda i: (0, 0)),
                  pl.BlockSpec((1, L2), lambda i: (0, 0)),
                  pl.BlockSpec((RWKV_LORA, 2 * W), lambda i: (0, 0)), vec, vec,
                  pl.BlockSpec((RWKV_LORA, W), lambda i: (0, 0)), vec, vec, vec, vec, vec,
                  pl.BlockSpec((G, G), lambda i: (0, 0)),
                  pl.BlockSpec((tt, tt), lambda i: (0, 0))],
        out_specs=pl.BlockSpec((tt, W), lambda i: (tb(i), 0)),
        out_shape=jax.ShapeDtypeStruct((tp, W), BF16),
        scratch_shapes=[pltpu.VMEM((8, W), F32), pltpu.VMEM((1, L2), F32),
                        pltpu.VMEM((W // G, G, G), F32)] + [big] * 11,
        compiler_params=_params(("arbitrary",)),
        name="rwkv7",
    )(u, u, u, u, mu, mu_lora, wwa, w0, a0, g2, k_k, k_a, r_k, ln_w, ln_b, seg, tril)


def _cumsum_rows(x, tril_bf16):
    hi = x.astype(BF16)
    lo = (x - hi.astype(F32)).astype(BF16)
    return (jnp.dot(tril_bf16, hi, preferred_element_type=F32)
            + jnp.dot(tril_bf16, lo, preferred_element_type=F32))


def _gla_kernel(q_ref, k_ref, v_ref, g_ref, hn_ref, wgd_ref, up_ref, gb_ref, nw_ref, tril_ref, ones_ref,
                o_ref, state_ref, gk_ref, qt_ref, kt_ref, a_ref, dec_ref, *, tt):
    i = pl.program_id(0)
    C = GLA_CHUNK
    SB = GLA_SUB
    DK = GLA_DK
    DV = GLA_DV
    NEG = -jnp.inf

    @pl.when(i == 0)
    def _():
        state_ref[...] = jnp.zeros_like(state_ref)

    gd = jnp.dot(hn_ref[...], wgd_ref[...], preferred_element_type=F32)
    z = _dot(gd, up_ref[...]) + gb_ref[...]
    gk_ref[...] = -_softplus(-z) * (LOG2E / GLA_GATE_NORM)

    crow = lax.broadcasted_iota(jnp.int32, (C, DK), 0)
    sub_row = crow % SB
    lane128 = lax.broadcasted_iota(jnp.int32, (C, 128), 1)
    row128 = lax.broadcasted_iota(jnp.int32, (C, 128), 0)
    diag_mask = (lane128 // SB == row128 // SB) & (lane128 < C)
    scale = DK ** -0.5

    heads = range(GLA_HEADS)
    chunks = range(tt // C)
    blocks = [(c, hd) for c in chunks for hd in heads]
    nsub = C // SB

    def row_sel(x, jj):
        return jnp.concatenate(
            [jnp.broadcast_to(x[s * SB + jj:s * SB + jj + 1], (SB, DK)) for s in range(nsub)], axis=0)

    for start in range(0, len(blocks), GLA_BATCH):
        batch = blocks[start:start + GLA_BATCH]
        idx = [(slice(c * C, (c + 1) * C), slice(hd * DK, (hd + 1) * DK)) for c, hd in batch]
        b = [_cumsum_rows(gk_ref[rs, ks], tril_ref[...]) for rs, ks in idx]
        q = [q_ref[rs, ks] * scale for rs, ks in idx]
        k = [jnp.where((crow + (i * tt + c * C)) >= PAD, k_ref[rs, ks], 0.0)
             for (c, _), (rs, ks) in zip(batch, idx)]
        for n, ((c, _), (rs, ks)) in enumerate(zip(batch, idx)):
            b_last = b[n][C - 1:C]
            qt_ref[rs, ks] = q[n] * jnp.exp2(b[n])
            kt_ref[rs, ks] = k[n] * jnp.exp2(b_last - b[n])
            dec_ref[c:c + 1, ks] = jnp.exp2(b_last)

        pieces = [[jnp.zeros((SB, 128), F32)] for _ in batch]
        for sb in range(1, nsub):
            r0 = sb * SB
            for n in range(len(batch)):
                ref_b = b[n][r0 - 1:r0]
                qh = q[n][r0:r0 + SB] * jnp.exp2(b[n][r0:r0 + SB] - ref_b)
                kh = k[n][0:r0] * jnp.exp2(ref_b - b[n][0:r0])
                kh = jnp.concatenate([kh, jnp.zeros((128 - r0, DK), F32)], axis=0)
                pieces[n].append(_dot_nt(qh, kh))

        diag = [jnp.zeros((C, 128), F32) for _ in batch]
        for jj in range(SB):
            for n in range(len(batch)):
                e = jnp.exp2(jnp.where(sub_row >= jj, b[n] - row_sel(b[n], jj), NEG))
                col = _dot(q[n] * e * row_sel(k[n], jj), ones_ref[...])
                diag[n] = jnp.where(lane128 % SB == jj, col, diag[n])
        for n, (c, hd) in enumerate(batch):
            a_ref[c * C:(c + 1) * C, hd * 128:(hd + 1) * 128] = (
                jnp.concatenate(pieces[n], axis=0) + jnp.where(diag_mask, diag[n], 0.0))

    for c in chunks:
        rs = slice(c * C, (c + 1) * C)
        valid_v = (lax.broadcasted_iota(jnp.int32, (C, DV), 0) + (i * tt + c * C)) >= PAD
        ksl = [slice(hd * DK, (hd + 1) * DK) for hd in heads]
        vsl = [slice(hd * DV, (hd + 1) * DV) for hd in heads]
        v = [jnp.where(valid_v, v_ref[rs, vs], 0.0) for vs in vsl]
        o = [_dot_nt(qt_ref[rs, ksl[hd]], state_ref[hd]) for hd in heads]
        o = [o[hd] + _dot(a_ref[rs, hd * 128:(hd + 1) * 128],
                          jnp.concatenate([v[hd], jnp.zeros((128 - C, DV), F32)], axis=0))
             for hd in heads]
        for hd in heads:
            on = o[hd] * lax.rsqrt(jnp.mean(o[hd] * o[hd], axis=-1, keepdims=True) + NORM_EPS) * nw_ref[...]
            gate = g_ref[rs, vsl[hd]]
            o_ref[rs, vsl[hd]] = (on * (gate * jax.nn.sigmoid(gate))).astype(BF16)
        upd = [_dot_tn(v[hd], kt_ref[rs, ksl[hd]]) for hd in heads]
        for hd in heads:
            state_ref[hd] = state_ref[hd] * dec_ref[c:c + 1, ksl[hd]] + upd[hd]


def _gla(ug, hn, w_gd, gk_up, gk_b, norm_w, tril, ones):
    tp = ug.shape[0]
    tt = HEAD
    tb = _time_block(tp // tt)
    return pl.pallas_call(
        functools.partial(_gla_kernel, tt=tt),
        grid=(tp // tt,),
        in_specs=[pl.BlockSpec((tt, GLA_KW), lambda i: (tb(i), 0)),
                  pl.BlockSpec((tt, GLA_KW), lambda i: (tb(i), 1)),
                  pl.BlockSpec((tt, GLA_VW), lambda i: (tb(i), 1)),
                  pl.BlockSpec((tt, GLA_VW), lambda i: (tb(i), 2)),
                  pl.BlockSpec((tt, D_MODEL), lambda i: (tb(i), 0)),
                  pl.BlockSpec((D_MODEL, 128), lambda i: (0, 0)),
                  pl.BlockSpec((128, GLA_KW), lambda i: (0, 0)),
                  pl.BlockSpec((1, GLA_KW), lambda i: (0, 0)),
                  pl.BlockSpec((1, GLA_DV), lambda i: (0, 0)),
                  pl.BlockSpec((GLA_CHUNK, GLA_CHUNK), lambda i: (0, 0)),
                  pl.BlockSpec((GLA_DK, 128), lambda i: (0, 0))],
        out_specs=pl.BlockSpec((tt, GLA_VW), lambda i: (tb(i), 0)),
        out_shape=jax.ShapeDtypeStruct((tp, GLA_VW), BF16),
        scratch_shapes=[pltpu.VMEM((GLA_HEADS, GLA_DV, GLA_DK), F32),
                        pltpu.VMEM((tt, GLA_KW), F32),
                        pltpu.VMEM((tt, GLA_KW), F32),
                        pltpu.VMEM((tt, GLA_KW), F32),
                        pltpu.VMEM((tt, GLA_HEADS * 128), F32),
                        pltpu.VMEM((8, GLA_KW), F32)],
        compiler_params=_params(("arbitrary",)),
        name="gla",
    )(ug, ug, ug, ug, hn, w_gd, gk_up, gk_b, norm_w, tril, ones)


def _block_diag_groups(w):
    nb = LRU_GROUP // LRU_BLOCK
    w4 = w.reshape(-1, nb, LRU_BLOCK, LRU_BLOCK)
    out = jnp.einsum("gaij,ab->gaibj", w4, jnp.eye(nb, dtype=w.dtype))
    return out.reshape(-1, LRU_GROUP, LRU_GROUP)


def _row(v):
    return v.reshape(1, -1).astype(F32)


def _ffn(h, hn, layer, w_gate, w_val, conv_w, conv_b, w_down, gpost, gnext):
    cw = jnp.zeros((8, D_FF), F32).at[:FFN_CONV].set(conv_w[layer])
    act = _ffn_up(hn, w_gate, w_val, layer, cw, _row(conv_b[layer]))
    return _mm_res([act], [(w_down.astype(BF16), layer)], h, gpost, gnext, RES_ROW_TILES_FFN,
                   f"ffn_down{layer}")


def kernel(x, meta_tokens, mix_pre_norm, mix_post_norm, ffn_pre_norm, ffn_post_norm, ab_w_in, lru_conv_w, lru_conv_b, lru_gate_a_w, lru_gate_a_b, lru_gate_x_w, lru_gate_x_b, lru_lambda, rwkv_shift_mu, rwkv_w0, rwkv_w2, rwkv_a0, rwkv_a2, rwkv_g2, rwkv_k_k, rwkv_k_a, rwkv_r_k, rwkv_ln_w, rwkv_ln_b, ab_w_out, gla_w_in, gla_gk_up, gla_gk_b, gla_norm_w, gla_w_out, ffn_w_gate, ffn_w_val, ffn_conv_w, ffn_conv_b, ffn_w_down):
    batch, seq, _ = x.shape
    assert batch == 1 and seq % HEAD == 0
    W = RWKV_WIDTH
    head = jnp.concatenate([jnp.zeros((PAD, D_MODEL), F32), meta_tokens.astype(F32)], axis=0)
    hn = _embed(head, x[0], _row(mix_pre_norm[0]))

    u = _mm(hn, ab_w_in, ab_w_in.shape[2], MM_COL_TILE, "ab_in")
    cw = jnp.zeros((8, LRU_WIDTH), F32).at[:LRU_CONV].set(lru_conv_w[0])
    lru_out = _lru(u, cw, _row(lru_conv_b[0]),
                   _block_diag_groups(lru_gate_a_w[0]).astype(BF16), _row(lru_gate_a_b[0]),
                   _block_diag_groups(lru_gate_x_w[0]).astype(BF16), _row(lru_gate_x_b[0]),
                   _row(lru_lambda[0]))
    half = RWKV_LORA // 2
    wwa = jnp.zeros((RWKV_LORA, 2 * W), F32)
    wwa = wwa.at[:half, :W].set(rwkv_w2[0]).at[half:, W:].set(rwkv_a2[0]).astype(BF16)
    seg = jnp.kron(jnp.eye(RWKV_GROUP // RWKV_HEAD_DIM, dtype=F32),
                   jnp.ones((RWKV_HEAD_DIM, RWKV_HEAD_DIM), F32)).astype(BF16)
    mu = jnp.zeros((8, W), F32).at[:3].set(rwkv_shift_mu[0, :3 * W].reshape(3, W))
    chunk_tril = jnp.kron(jnp.eye(HEAD // RWKV_CHUNK, dtype=F32),
                          jnp.tril(jnp.ones((RWKV_CHUNK, RWKV_CHUNK), F32))).astype(BF16)
    rw_out = _rwkv(u, mu, _row(rwkv_shift_mu[0, 3 * W:]), wwa, _row(rwkv_w0[0]), _row(rwkv_a0[0]),
                   rwkv_g2[0].astype(BF16), _row(rwkv_k_k[0]), _row(rwkv_k_a[0]), _row(rwkv_r_k[0]),
                   _row(rwkv_ln_w[0]), _row(rwkv_ln_b[0]), seg, chunk_tril)
    w_out = ab_w_out[0].astype(BF16)
    h, hn = _mm_res([lru_out, rw_out], [w_out[:LRU_WIDTH], w_out[LRU_WIDTH:]], (x[0], head),
                    _row(mix_post_norm[0]), _row(ffn_pre_norm[0]), (HEAD,), "ab_out")
    h, hn = _ffn(h, hn, 0, ffn_w_gate, ffn_w_val, ffn_conv_w, ffn_conv_b, ffn_w_down,
                 _row(ffn_post_norm[0]), _row(mix_pre_norm[1]))

    ug = _mm(hn, jnp.swapaxes(gla_w_in, 1, 2), GLA_MAIN, MM_COL_TILE, "gla_in", transposed=True)
    w_gd = jnp.zeros((D_MODEL, 128), F32).at[:, :GLA_GATE_RANK].set(gla_w_in[0, :, GLA_MAIN:]).astype(BF16)
    up = jnp.zeros((128, GLA_KW), F32).at[:GLA_GATE_RANK].set(gla_gk_up[0]).astype(BF16)
    tril = jnp.tril(jnp.ones((GLA_CHUNK, GLA_CHUNK), F32)).astype(BF16)
    ones = jnp.ones((GLA_DK, 128), BF16)
    o = _gla(ug, hn, w_gd, up, _row(gla_gk_b[0]), _row(gla_norm_w[0]), tril, ones)
    h, hn = _mm_res([o], [gla_w_out[0].astype(BF16)], h,
                    _row(mix_post_norm[1]), _row(ffn_pre_norm[1]), RES_ROW_TILES_MIX, "gla_out")
    out, _ = _ffn(h, hn, 1, ffn_w_gate, ffn_w_val, ffn_conv_w, ffn_conv_b, ffn_w_down,
                  _row(ffn_post_norm[1]), None)
    return out[None]
```

```python
import functools
import math

import jax
import jax.numpy as jnp
from jax import lax
from jax.experimental import pallas as pl
from jax.experimental.pallas import tpu as pltpu

F32 = jnp.float32
BF16 = jnp.bfloat16

D_MODEL = 2048
N_META = 16
NORM_EPS = 1e-6
HEAD = 256
PAD = HEAD - N_META
LRU_WIDTH = 1024
LRU_BLOCK = 64
LRU_GROUP = 256
LRU_CONV = 4
LRU_C = 8.0
RWKV_WIDTH = 1024
RWKV_HEAD_DIM = 64
RWKV_LORA = 128
RWKV_GN_EPS = 64e-5
RWKV_CHUNK = 64
RWKV_GROUP = 128
GLA_HEADS = 4
GLA_DK = 256
GLA_DV = 512
GLA_KW = GLA_HEADS * GLA_DK
GLA_VW = GLA_HEADS * GLA_DV
GLA_GATE_RANK = 16
GLA_GATE_NORM = 16.0
GLA_CHUNK = 64
GLA_SUB = 8
GLA_BATCH = 4
GLA_MAIN = 2 * GLA_KW + 2 * GLA_VW
D_FF = 5632
FFN_CONV = 3
LOG2E = 1.4426950408889634
VMEM_LIMIT = 56 * 1024 * 1024

MM_ROW_TILES = (1664, 768, 256)
MM_COL_TILE = 768
FFN_ROW_TILES = (1280, 768, 256)
FFN_COL_TILE = 512
RES_COL_TILE = 512
RES_ROW_TILES_MIX = (416, 256)
RES_ROW_TILES_FFN = (256,)


def _pick(n, candidates):
    for c in candidates:
        if n % c == 0:
            return c
    raise ValueError(f"no tile for {n} in {candidates}")


def _params(sem):
    return pltpu.CompilerParams(dimension_semantics=sem, vmem_limit_bytes=VMEM_LIMIT)


def _dot(a, b):
    return jnp.dot(a.astype(BF16), b.astype(BF16), preferred_element_type=F32)


def _dot_nt(a, b):
    return lax.dot_general(a.astype(BF16), b.astype(BF16), (((1,), (1,)), ((), ())),
                           preferred_element_type=F32)


def _dot_tn(a, b):
    return lax.dot_general(a.astype(BF16), b.astype(BF16), (((0,), (0,)), ((), ())),
                           preferred_element_type=F32)


def _softplus(x):
    return jnp.maximum(x, 0.0) + jnp.log1p(jnp.exp(-jnp.abs(x)))


def _one_minus_exp2x(x):
    t = jnp.tanh(x)
    return -2.0 * t / (1.0 - t)


def _rms(x):
    return x * lax.rsqrt(jnp.mean(x * x, axis=-1, keepdims=True) + NORM_EPS)


def _row_ids(shape, base):
    return lax.broadcasted_iota(jnp.int32, shape, 0) + base


def _shift_rows(x, s, prev8):
    rolled = pltpu.roll(x, s, 0)
    rows = lax.broadcasted_iota(jnp.int32, prev8.shape, 0)
    top = jnp.where(rows < s, pltpu.roll(prev8, s, 0), rolled[0:8])
    return jnp.concatenate([top, rolled[8:]], axis=0)


def _time_block(nb):
    return lambda i: ((i + nb - 1) % nb)


def _embed_kernel(head_ref, x_ref, g_ref, hn_ref):
    i = pl.program_id(0)
    xb = jnp.where(i == pl.num_programs(0) - 1, head_ref[...], x_ref[...])
    hn_ref[...] = (_rms(xb) * g_ref[...]).astype(BF16)


def _embed(head, x2d, gain):
    seq = x2d.shape[0]
    tp = seq + HEAD
    nb = tp // HEAD
    blk = pl.BlockSpec((HEAD, D_MODEL), lambda i: (i, 0))
    return pl.pallas_call(
        _embed_kernel,
        grid=(nb,),
        in_specs=[pl.BlockSpec((HEAD, D_MODEL), lambda i: (0, 0)),
                  pl.BlockSpec((HEAD, D_MODEL), lambda i: (jnp.minimum(i, nb - 2), 0)),
                  pl.BlockSpec((1, D_MODEL), lambda i: (0, 0))],
        out_specs=blk,
        out_shape=jax.ShapeDtypeStruct((tp, D_MODEL), BF16),
        compiler_params=_params(("arbitrary",)),
        name="embed",
    )(head, x2d, gain)


def _mm_kernel(x_ref, w_ref, o_ref):
    o_ref[...] = jnp.dot(x_ref[...], w_ref[...].astype(BF16), preferred_element_type=F32)


def _mm_nt_kernel(x_ref, w_ref, o_ref):
    o_ref[...] = lax.dot_general(x_ref[...], w_ref[...].astype(BF16), (((1,), (1,)), ((), ())),
                                 preferred_element_type=F32)


def _mm(x, w3, n, tn, name, transposed=False):
    tp, k = x.shape
    tm = _pick(tp, MM_ROW_TILES)
    if transposed:
        wspec = pl.BlockSpec((None, tn, k), lambda i, j: (0, j, 0))
    else:
        wspec = pl.BlockSpec((None, k, tn), lambda i, j: (0, 0, j))
    return pl.pallas_call(
        _mm_nt_kernel if transposed else _mm_kernel,
        grid=(tp // tm, n // tn),
        in_specs=[pl.BlockSpec((tm, k), lambda i, j: (i, 0)), wspec],
        out_specs=pl.BlockSpec((tm, tn), lambda i, j: (i, j)),
        out_shape=jax.ShapeDtypeStruct((tp, n), F32),
        compiler_params=_params(("arbitrary", "arbitrary")),
        name=name,
    )(x, w3)


def _mm_res_kernel(*refs, n_x, tm, tn, seq, has_next, split_h):
    x_refs = refs[:n_x]
    w_refs = refs[n_x:2 * n_x]
    n_h = 2 if split_h else 1
    h_refs = refs[2 * n_x:2 * n_x + n_h]
    gpost_ref = refs[2 * n_x + n_h]
    rest = refs[2 * n_x + n_h + 1:]
    if has_next:
        gnext_ref, ho_ref, hn_ref, acc_a, acc_b = rest
    else:
        ho_ref, acc_a, acc_b = rest
    s = pl.program_id(0)
    nt = D_MODEL // tn

    @pl.when(s == 0)
    def _():
        acc_b[...] = jnp.zeros_like(acc_b)

    def residual(cs):
        if not split_h:
            return h_refs[0][:, cs]
        return jnp.where(s == pl.num_programs(0) - 1, h_refs[1][:, cs], h_refs[0][:, cs])

    def step(cur_ref, prev_ref):
        def matmul_cols(t):
            cs = slice(t * tn, (t + 1) * tn)
            part = jnp.dot(x_refs[0][...], w_refs[0][:, cs], preferred_element_type=F32)
            for xr, wr in zip(x_refs[1:], w_refs[1:]):
                part = part + jnp.dot(xr[...], wr[:, cs], preferred_element_type=F32)
            cur_ref[t] = part

        matmul_cols(0)
        ssq = jnp.zeros((tm, 1), F32)
        for t in range(nt):
            m = prev_ref[t]
            ssq = ssq + jnp.sum(m * m, axis=-1, keepdims=True)
        scale = lax.rsqrt(ssq * (1.0 / D_MODEL) + NORM_EPS)
        rows = _row_ids((tm, 1), (s - 1) * tm)
        scale = jnp.where((rows < seq) | (rows >= seq + PAD), scale, 0.0)
        matmul_cols(1)
        ssq2 = jnp.zeros((tm, 1), F32)
        for t in range(nt):
            cs = slice(t * tn, (t + 1) * tn)
            hnew = residual(cs) + prev_ref[t] * scale * gpost_ref[:, cs]
            ho_ref[:, cs] = hnew
            if has_next:
                ssq2 = ssq2 + jnp.sum(hnew * hnew, axis=-1, keepdims=True)
        matmul_cols(2)
        if has_next:
            scale2 = lax.rsqrt(ssq2 * (1.0 / D_MODEL) + NORM_EPS)
            for t in range(nt):
                cs = slice(t * tn, (t + 1) * tn)
                hn_ref[:, cs] = (ho_ref[:, cs] * scale2 * gnext_ref[:, cs]).astype(BF16)
        for t in range(3, nt):
            matmul_cols(t)

    @pl.when(s % 2 == 0)
    def _():
        step(acc_a, acc_b)

    @pl.when(s % 2 == 1)
    def _():
        step(acc_b, acc_a)


def _mm_res(xs, ws, h, gpost, gnext, tm_candidates, name):
    split_h = isinstance(h, tuple)
    tp = xs[0].shape[0]
    seq = tp - HEAD
    kdim = xs[0].shape[1]
    has_next = gnext is not None
    rows_out = tp if has_next else seq
    tm = _pick(rows_out, tm_candidates)
    tn = RES_COL_TILE
    n = rows_out // tm
    row = pl.BlockSpec((tm, D_MODEL), lambda s: (jnp.maximum(s - 1, 0), 0))
    gain = pl.BlockSpec((1, D_MODEL), lambda s: (0, 0))
    if split_h:
        assert tm == HEAD and has_next
        h_args = list(h)
        h_specs = [pl.BlockSpec((tm, D_MODEL), lambda s: (jnp.clip(s - 1, 0, n - 2), 0)),
                   pl.BlockSpec((HEAD, D_MODEL), lambda s: (0, 0))]
    else:
        h_args, h_specs = [h], [row]
    w_args, w_specs = [], []
    for w in ws:
        if isinstance(w, tuple):
            w_args.append(w[0])
            w_specs.append(pl.BlockSpec((None, kdim, D_MODEL), lambda s, layer=w[1]: (layer, 0, 0),
                                        pipeline_mode=pl.Buffered(1)))
        else:
            w_args.append(w)
            w_specs.append(pl.BlockSpec((kdim, D_MODEL), lambda s: (0, 0), pipeline_mode=pl.Buffered(1)))
    in_specs = ([pl.BlockSpec((tm, kdim), lambda s: (jnp.minimum(s, n - 1), 0)) for _ in xs]
                + w_specs + h_specs + [gain] + ([gain] if has_next else []))
    out_specs = [row] + ([row] if has_next else [])
    out_shape = [jax.ShapeDtypeStruct((rows_out, D_MODEL), F32)]
    if has_next:
        out_shape.append(jax.ShapeDtypeStruct((tp, D_MODEL), BF16))
    args = list(xs) + w_args + h_args + [gpost] + ([gnext] if has_next else [])
    out = pl.pallas_call(
        functools.partial(_mm_res_kernel, n_x=len(xs), tm=tm, tn=tn, seq=seq, has_next=has_next,
                          split_h=split_h),
        grid=(n + 1,),
        in_specs=in_specs,
        out_specs=out_specs,
        out_shape=out_shape,
        scratch_shapes=[pltpu.VMEM((D_MODEL // tn, tm, tn), F32)] * 2,
        compiler_params=_params(("arbitrary",)),
        name=name,
    )(*args)
    return out if has_next else (out[0], None)


def _ffn_up_kernel(x_ref, xh_ref, wg_ref, wv_ref, cw_ref, cb_ref, o_ref):
    x = x_ref[...]
    gate_ext = jnp.dot(jnp.concatenate([xh_ref[...], x], axis=0), wg_ref[...].astype(BF16),
                       preferred_element_type=F32)
    gate = gate_ext[16:]
    halo = gate_ext[8:16]
    val = jnp.dot(x, wv_ref[...].astype(BF16), preferred_element_type=F32)
    cw = cw_ref[...]
    conv = (cw[2:3] * gate + cw[1:2] * _shift_rows(gate, 1, halo)
            + cw[0:1] * _shift_rows(gate, 2, halo) + cb_ref[...])
    o_ref[...] = (jax.nn.gelu(conv) * val).astype(BF16)


def _ffn_up(hn, w_gate, w_val, layer, conv_w, conv_b):
    tp = hn.shape[0]
    tm = _pick(tp, FFN_ROW_TILES)
    tn = FFN_COL_TILE
    halo_blocks = tm // 16
    nhalo = tp // 16
    wspec = pl.BlockSpec((None, D_MODEL, tn), lambda i, j: (layer, 0, j))
    return pl.pallas_call(
        _ffn_up_kernel,
        grid=(tp // tm, D_FF // tn),
        in_specs=[pl.BlockSpec((tm, D_MODEL), lambda i, j: (i, 0)),
                  pl.BlockSpec((16, D_MODEL), lambda i, j: ((i * halo_blocks + nhalo - 1) % nhalo, 0)),
                  wspec, wspec,
                  pl.BlockSpec((8, tn), lambda i, j: (0, j)),
                  pl.BlockSpec((1, tn), lambda i, j: (0, j))],
        out_specs=pl.BlockSpec((tm, tn), lambda i, j: (i, j)),
        out_shape=jax.ShapeDtypeStruct((tp, D_FF), BF16),
        compiler_params=_params(("arbitrary", "arbitrary")),
        name="ffn_up",
    )(hn, hn, w_gate, w_val, conv_w, conv_b)


def _lru_kernel(lx_ref, ly_ref, cw_ref, cb_ref, wa_ref, ba_ref, wx_ref, bx_ref, lam_ref,
                o_ref, halo_ref, state_ref, *, tt):
    i = pl.program_id(0)

    @pl.when(i == 0)
    def _():
        halo_ref[...] = jnp.zeros_like(halo_ref)
        state_ref[...] = jnp.zeros_like(state_ref)

    lx = lx_ref[...]
    prev = halo_ref[...]
    cw = cw_ref[...]
    xc = (cw[3:4] * lx + cw[2:3] * _shift_rows(lx, 1, prev) + cw[1:2] * _shift_rows(lx, 2, prev)
          + cw[0:1] * _shift_rows(lx, 3, prev) + cb_ref[...])
    halo_ref[...] = lx[tt - 8:tt]

    xb = xc.astype(BF16)
    ga, gx = [], []
    for g in range(LRU_WIDTH // LRU_GROUP):
        blk = xb[:, g * LRU_GROUP:(g + 1) * LRU_GROUP]
        ga.append(jnp.dot(blk, wa_ref[g], preferred_element_type=F32))
        gx.append(jnp.dot(blk, wx_ref[g], preferred_element_type=F32))
    r = jax.nn.sigmoid(jnp.concatenate(ga, axis=1) + ba_ref[...])
    gi = jax.nn.sigmoid(jnp.concatenate(gx, axis=1) + bx_ref[...])
    log_a = (-LRU_C) * r * _softplus(-lam_ref[...])
    a = jnp.exp(log_a)
    rows = _row_ids(a.shape, i * tt)
    b = jnp.where(rows >= PAD, jnp.sqrt(_one_minus_exp2x(log_a)) * (gi * xc), 0.0)

    a = a.reshape(tt // 8, 8, LRU_WIDTH)
    b = b.reshape(tt // 8, 8, LRU_WIDTH)
    sub = lax.broadcasted_iota(jnp.int32, a.shape, 1)
    for d in (1, 2, 4):
        a_sh = jnp.where(sub < d, 1.0, pltpu.roll(a, d, 1))
        b_sh = jnp.where(sub < d, 0.0, pltpu.roll(b, d, 1))
        b = b + a * b_sh
        a = a * a_sh
    carry = state_ref[...]
    groups = []
    for g in range(tt // 8):
        h_g = b[g] + a[g] * carry
        groups.append(h_g)
        carry = h_g[7:8]
    state_ref[...] = carry
    h = jnp.concatenate(groups, axis=0)
    o_ref[...] = (h * jax.nn.gelu(ly_ref[...])).astype(BF16)


def _lru(u, conv_w, conv_b, wa, ba, wx, bx, lam):
    tp = u.shape[0]
    tt = HEAD
    tb = _time_block(tp // tt)
    vec = pl.BlockSpec((1, LRU_WIDTH), lambda i: (0, 0))
    wspec = pl.BlockSpec((LRU_WIDTH // LRU_GROUP, LRU_GROUP, LRU_GROUP), lambda i: (0, 0, 0))
    return pl.pallas_call(
        functools.partial(_lru_kernel, tt=tt),
        grid=(tp // tt,),
        in_specs=[pl.BlockSpec((tt, LRU_WIDTH), lambda i: (tb(i), 0)),
                  pl.BlockSpec((tt, LRU_WIDTH), lambda i: (tb(i), 1)),
                  pl.BlockSpec((8, LRU_WIDTH), lambda i: (0, 0)), vec,
                  wspec, vec, wspec, vec, vec],
        out_specs=pl.BlockSpec((tt, LRU_WIDTH), lambda i: (tb(i), 0)),
        out_shape=jax.ShapeDtypeStruct((tp, LRU_WIDTH), BF16),
        scratch_shapes=[pltpu.VMEM((8, LRU_WIDTH), F32), pltpu.VMEM((1, LRU_WIDTH), F32)],
        compiler_params=_params(("arbitrary",)),
        name="rglru",
    )(u, u, conv_w, conv_b, wa, ba, wx, bx, lam)


def _rwkv_kernel(r_ref, k_ref, vin_ref, lo_ref, mu_ref, mul_ref, wwa_ref, w0_ref, a0_ref, g2_ref,
                 kk_ref, ka_ref, rk_ref, lnw_ref, lnb_ref, seg_ref, tril_ref, o_ref,
                 last_ref, lastl_ref, state_ref, rt_ref, kt_ref, kb_ref, pb_ref, v_ref, gam_ref, y_ref,
                 x_ref, bv_ref, qkv_ref, qp_ref, *, tt):
    i = pl.program_id(0)
    C = RWKV_CHUNK
    G = RWKV_GROUP
    W = RWKV_WIDTH
    NG = W // G

    @pl.when(i == 0)
    def _():
        last_ref[...] = jnp.zeros_like(last_ref)
        lastl_ref[...] = jnp.zeros_like(lastl_ref)
        state_ref[...] = jnp.zeros_like(state_ref)

    def token_shift(x, mu, last):
        return x + (_shift_rows(x, 1, jnp.tile(last, (8, 1))) - x) * mu

    r_in, k_in, v_in, lo_in = r_ref[...], k_ref[...], vin_ref[...], lo_ref[...]
    r = token_shift(r_in, mu_ref[0:1], last_ref[0:1])
    k = token_shift(k_in, mu_ref[1:2], last_ref[1:2])
    v = token_shift(v_in, mu_ref[2:3], last_ref[2:3])
    lo = token_shift(lo_in, mul_ref[...], lastl_ref[...])
    last_ref[0:1] = r_in[tt - 1:tt]
    last_ref[1:2] = k_in[tt - 1:tt]
    last_ref[2:3] = v_in[tt - 1:tt]
    lastl_ref[...] = lo_in[tt - 1:tt]

    lora = lo[:, 0:RWKV_LORA]
    lane = lax.broadcasted_iota(jnp.int32, lora.shape, 1)
    lora = jnp.where(lane < RWKV_LORA // 2, jnp.tanh(lora), lora)
    wa = _dot(lora, wwa_ref[...])
    ell = (-math.exp(-0.5) * LOG2E) * jax.nn.sigmoid(w0_ref[...] + wa[:, 0:W])
    a = jax.nn.sigmoid(a0_ref[...] + wa[:, W:2 * W])
    g = _dot(jax.nn.sigmoid(lo[:, RWKV_LORA:]), g2_ref[...])

    seg = seg_ref[...]

    def head_sums(x):
        return jnp.concatenate([_dot(x[:, t * G:(t + 1) * G], seg) for t in range(NG)], axis=1)

    kk = k * kk_ref[...]
    kk = kk / jnp.maximum(jnp.sqrt(head_sums(kk * kk)), 1e-12)
    rows = _row_ids(k.shape, i * tt)
    k = jnp.where(rows >= PAD, k * (1.0 + (a - 1.0) * ka_ref[...]), 0.0)
    bonus = head_sums(r * k * rk_ref[...]) * v

    b = _cumsum_rows(ell, tril_ref[...])
    eb = jnp.exp2(-b)
    gam = jnp.exp2(b)
    rt_ref[...] = r * gam
    kt_ref[...] = kk * jnp.exp2(b - ell)
    kb_ref[...] = k * eb
    pb_ref[...] = kk * a * eb
    v_ref[...] = v
    gam_ref[...] = gam

    crow = lax.broadcasted_iota(jnp.int32, (C, G), 0)
    ccol = lax.broadcasted_iota(jnp.int32, (C, G), 1) % C
    strict = ccol < crow
    incl = ccol <= crow
    eye = jnp.where(ccol == crow, 1.0, 0.0)
    groups = range(NG)

    def bd(x):
        xb = x.astype(BF16)
        return jnp.concatenate([xb, xb], axis=0) * seg

    chunks = range(tt // C)

    blocks = [(slice(c * C, (c + 1) * C), slice(gi * G, (gi + 1) * G)) for c in chunks for gi in groups]

    keys = [jnp.concatenate([bd(pb_ref[rs, ls]), bd(kb_ref[rs, ls])], axis=0) for rs, ls in blocks]
    sc = [_dot_nt(jnp.concatenate([kt_ref[rs, ls], rt_ref[rs, ls]], axis=0), keys[n])
          for n, (rs, ls) in enumerate(blocks)]
    A = [jnp.where(strict, m[0:C, 0:G], 0.0) for m in sc]
    for n, (rs, ls) in enumerate(blocks):
        both = jnp.concatenate([jnp.where(strict, sc[n][0:C, G:2 * G], 0.0),
                                jnp.where(incl, sc[n][C:2 * C, G:2 * G], 0.0)], axis=0)
        prod = _dot(both, bd(v_ref[rs, ls]))
        bv_ref[rs, ls] = prod[0:C]
        qkv_ref[rs, ls] = prod[C:2 * C]
        qp_ref[rs, ls] = jnp.where(incl, sc[n][C:2 * C, 0:G], 0.0)
    X = [eye - m for m in A]
    P = [_dot(m, bd(m)) for m in A]
    n_pow = 2
    while n_pow < C:
        n_pow *= 2
        if n_pow < C:
            xp = [_dot(jnp.concatenate([x, p], axis=0), bd(p)) for x, p in zip(X, P)]
            X = [x + m[0:C] for x, m in zip(X, xp)]
            P = [m[C:2 * C] for m in xp]
        else:
            X = [x + _dot(x, bd(p)) for x, p in zip(X, P)]
    for n, (rs, ls) in enumerate(blocks):
        x_ref[rs, ls] = X[n]

    S = [state_ref[gi] for gi in groups]
    for c in chunks:
        rs = slice(c * C, (c + 1) * C)
        cols = [slice(gi * G, (gi + 1) * G) for gi in groups]
        krs = [_dot_nt(jnp.concatenate([kt_ref[rs, ls], rt_ref[rs, ls]], axis=0), S[gi])
               for gi, ls in enumerate(cols)]
        rts = [m[C:2 * C] for m in krs]
        U = [_dot(x_ref[rs, ls], bd(krs[gi][0:C] + bv_ref[rs, ls])) for gi, ls in enumerate(cols)]
        gg = [gam_ref[(c + 1) * C - 1:(c + 1) * C, ls] for ls in cols]
        upd = [_dot_tn(jnp.concatenate([v_ref[rs, ls], U[gi]], axis=0),
                       jnp.concatenate([kb_ref[rs, ls] * gg[gi], -(pb_ref[rs, ls] * gg[gi])], axis=0))
               for gi, ls in enumerate(cols)]
        for gi, ls in enumerate(cols):
            y_ref[rs, ls] = rts[gi] + qkv_ref[rs, ls] - _dot(qp_ref[rs, ls], bd(U[gi]))
        S = [S[gi] * gg[gi] + upd[gi] * seg for gi in groups]
    for gi in groups:
        state_ref[gi] = S[gi]

    y = y_ref[...]
    inv_n = 1.0 / RWKV_HEAD_DIM
    mu = head_sums(y) * inv_n
    dlt = y - mu
    var = head_sums(dlt * dlt) * inv_n
    yn = dlt * lax.rsqrt(var + RWKV_GN_EPS) * lnw_ref[...] + lnb_ref[...]
    o_ref[...] = ((yn + bonus) * g).astype(BF16)


def _rwkv(u, mu, mu_lora, wwa, w0, a0, g2, k_k, k_a, r_k, ln_w, ln_b, seg, tril):
    tp = u.shape[0]
    tt = HEAD
    tb = _time_block(tp // tt)
    W = RWKV_WIDTH
    G = RWKV_GROUP
    L2 = 2 * RWKV_LORA
    vec = pl.BlockSpec((1, W), lambda i: (0, 0))
    first = 2 * LRU_WIDTH // W
    big = pltpu.VMEM((tt, W), F32)
    return pl.pallas_call(
        functools.partial(_rwkv_kernel, tt=tt),
        grid=(tp // tt,),
        in_specs=[pl.BlockSpec((tt, W), lambda i: (tb(i), first)),
                  pl.BlockSpec((tt, W), lambda i: (tb(i), first + 1)),
                  pl.BlockSpec((tt, W), lambda i: (tb(i), first + 2)),
                  pl.BlockSpec((tt, L2), lambda i: (tb(i), (first + 3) * W // L2)),
                  pl.BlockSpec((8, W), lambda i: (0, 0)),
                  pl.BlockSpec((1, L2), lambda i: (0, 0)),
                  pl.BlockSpec((RWKV_LORA, 2 * W), lambda i: (0, 0)), vec, vec,
                  pl.BlockSpec((RWKV_LORA, W), lambda i: (0, 0)), vec, vec, vec, vec, vec,
                  pl.BlockSpec((G, G), lambda i: (0, 0)),
                  pl.BlockSpec((tt, tt), lambda i: (0, 0))],
        out_specs=pl.BlockSpec((tt, W), lambda i: (tb(i), 0)),
        out_shape=jax.ShapeDtypeStruct((tp, W), BF16),
        scratch_shapes=[pltpu.VMEM((8, W), F32), pltpu.VMEM((1, L2), F32),
                        pltpu.VMEM((W // G, G, G), F32)] + [big] * 11,
        compiler_params=_params(("arbitrary",)),
        name="rwkv7",
    )(u, u, u, u, mu, mu_lora, wwa, w0, a0, g2, k_k, k_a, r_k, ln_w, ln_b, seg, tril)


def _cumsum_rows(x, tril_bf16):
    hi = x.astype(BF16)
    lo = (x - hi.astype(F32)).astype(BF16)
    return (jnp.dot(tril_bf16, hi, preferred_element_type=F32)
            + jnp.dot(tril_bf16, lo, preferred_element_type=F32))


def _gla_kernel(q_ref, k_ref, v_ref, g_ref, hn_ref, wgd_ref, up_ref, gb_ref, nw_ref, tril_ref, ones_ref,
                o_ref, state_ref, gk_ref, qt_ref, kt_ref, a_ref, dec_ref, *, tt):
    i = pl.program_id(0)
    C = GLA_CHUNK
    SB = GLA_SUB
    DK = GLA_DK
    DV = GLA_DV
    NEG = -jnp.inf

    @pl.when(i == 0)
    def _():
        state_ref[...] = jnp.zeros_like(state_ref)

    gd = jnp.dot(hn_ref[...], wgd_ref[...], preferred_element_type=F32)
    z = _dot(gd, up_ref[...]) + gb_ref[...]
    gk_ref[...] = -_softplus(-z) * (LOG2E / GLA_GATE_NORM)

    crow = lax.broadcasted_iota(jnp.int32, (C, DK), 0)
    sub_row = crow % SB
    lane128 = lax.broadcasted_iota(jnp.int32, (C, 128), 1)
    row128 = lax.broadcasted_iota(jnp.int32, (C, 128), 0)
    diag_mask = (lane128 // SB == row128 // SB) & (lane128 < C)
    scale = DK ** -0.5

    heads = range(GLA_HEADS)
    chunks = range(tt // C)
    blocks = [(c, hd) for c in chunks for hd in heads]
    nsub = C // SB

    def row_sel(x, jj):
        return jnp.concatenate(
            [jnp.broadcast_to(x[s * SB + jj:s * SB + jj + 1], (SB, DK)) for s in range(nsub)], axis=0)

    for start in range(0, len(blocks), GLA_BATCH):
        batch = blocks[start:start + GLA_BATCH]
        idx = [(slice(c * C, (c + 1) * C), slice(hd * DK, (hd + 1) * DK)) for c, hd in batch]
        b = [_cumsum_rows(gk_ref[rs, ks], tril_ref[...]) for rs, ks in idx]
        q = [q_ref[rs, ks] * scale for rs, ks in idx]
        k = [jnp.where((crow + (i * tt + c * C)) >= PAD, k_ref[rs, ks], 0.0)
             for (c, _), (rs, ks) in zip(batch, idx)]
        for n, ((c, _), (rs, ks)) in enumerate(zip(batch, idx)):
            b_last = b[n][C - 1:C]
            qt_ref[rs, ks] = q[n] * jnp.exp2(b[n])
            kt_ref[rs, ks] = k[n] * jnp.exp2(b_last - b[n])
            dec_ref[c:c + 1, ks] = jnp.exp2(b_last)

        pieces = [[jnp.zeros((SB, 128), F32)] for _ in batch]
        for sb in range(1, nsub):
            r0 = sb * SB
            for n in range(len(batch)):
                ref_b = b[n][r0 - 1:r0]
                qh = q[n][r0:r0 + SB] * jnp.exp2(b[n][r0:r0 + SB] - ref_b)
                kh = k[n][0:r0] * jnp.exp2(ref_b - b[n][0:r0])
                kh = jnp.concatenate([kh, jnp.zeros((128 - r0, DK), F32)], axis=0)
                pieces[n].append(_dot_nt(qh, kh))

        diag = [jnp.zeros((C, 128), F32) for _ in batch]
        for jj in range(SB):
            for n in range(len(batch)):
                e = jnp.exp2(jnp.where(sub_row >= jj, b[n] - row_sel(b[n], jj), NEG))
                col = _dot(q[n] * e * row_sel(k[n], jj), ones_ref[...])
                diag[n] = jnp.where(lane128 % SB == jj, col, diag[n])
        for n, (c, hd) in enumerate(batch):
            a_ref[c * C:(c + 1) * C, hd * 128:(hd + 1) * 128] = (
                jnp.concatenate(pieces[n], axis=0) + jnp.where(diag_mask, diag[n], 0.0))

    for c in chunks:
        rs = slice(c * C, (c + 1) * C)
        valid_v = (lax.broadcasted_iota(jnp.int32, (C, DV), 0) + (i * tt + c * C)) >= PAD
        ksl = [slice(hd * DK, (hd + 1) * DK) for hd in heads]
        vsl = [slice(hd * DV, (hd + 1) * DV) for hd in heads]
        v = [jnp.where(valid_v, v_ref[rs, vs], 0.0) for vs in vsl]
        o = [_dot_nt(qt_ref[rs, ksl[hd]], state_ref[hd]) for hd in heads]
        o = [o[hd] + _dot(a_ref[rs, hd * 128:(hd + 1) * 128],
                          jnp.concatenate([v[hd], jnp.zeros((128 - C, DV), F32)], axis=0))
             for hd in heads]
        for hd in heads:
            on = o[hd] * lax.rsqrt(jnp.mean(o[hd] * o[hd], axis=-1, keepdims=True) + NORM_EPS) * nw_ref[...]
            gate = g_ref[rs, vsl[hd]]
            o_ref[rs, vsl[hd]] = (on * (gate * jax.nn.sigmoid(gate))).astype(BF16)
        upd = [_dot_tn(v[hd], kt_ref[rs, ksl[hd]]) for hd in heads]
        for hd in heads:
            state_ref[hd] = state_ref[hd] * dec_ref[c:c + 1, ksl[hd]] + upd[hd]


def _gla(ug, hn, w_gd, gk_up, gk_b, norm_w, tril, ones):
    tp = ug.shape[0]
    tt = HEAD
    tb = _time_block(tp // tt)
    return pl.pallas_call(
        functools.partial(_gla_kernel, tt=tt),
        grid=(tp // tt,),
        in_specs=[pl.BlockSpec((tt, GLA_KW), lambda i: (tb(i), 0)),
                  pl.BlockSpec((tt, GLA_KW), lambda i: (tb(i), 1)),
                  pl.BlockSpec((tt, GLA_VW), lambda i: (tb(i), 1)),
                  pl.BlockSpec((tt, GLA_VW), lambda i: (tb(i), 2)),
                  pl.BlockSpec((tt, D_MODEL), lambda i: (tb(i), 0)),
                  pl.BlockSpec((D_MODEL, 128), lambda i: (0, 0)),
                  pl.BlockSpec((128, GLA_KW), lambda i: (0, 0)),
                  pl.BlockSpec((1, GLA_KW), lambda i: (0, 0)),
                  pl.BlockSpec((1, GLA_DV), lambda i: (0, 0)),
                  pl.BlockSpec((GLA_CHUNK, GLA_CHUNK), lambda i: (0, 0)),
                  pl.BlockSpec((GLA_DK, 128), lambda i: (0, 0))],
        out_specs=pl.BlockSpec((tt, GLA_VW), lambda i: (tb(i), 0)),
        out_shape=jax.ShapeDtypeStruct((tp, GLA_VW), BF16),
        scratch_shapes=[pltpu.VMEM((GLA_HEADS, GLA_DV, GLA_DK), F32),
                        pltpu.VMEM((tt, GLA_KW), F32),
                        pltpu.VMEM((tt, GLA_KW), F32),
                        pltpu.VMEM((tt, GLA_KW), F32),
                        pltpu.VMEM((tt, GLA_HEADS * 128), F32),
                        pltpu.VMEM((8, GLA_KW), F32)],
        compiler_params=_params(("arbitrary",)),
        name="gla",
    )(ug, ug, ug, ug, hn, w_gd, gk_up, gk_b, norm_w, tril, ones)


def _block_diag_groups(w):
    nb = LRU_GROUP // LRU_BLOCK
    w4 = w.reshape(-1, nb, LRU_BLOCK, LRU_BLOCK)
    out = jnp.einsum("gaij,ab->gaibj", w4, jnp.eye(nb, dtype=w.dtype))
    return out.reshape(-1, LRU_GROUP, LRU_GROUP)


def _row(v):
    return v.reshape(1, -1).astype(F32)


def _ffn(h, hn, layer, w_gate, w_val, conv_w, conv_b, w_down, gpost, gnext):
    cw = jnp.zeros((8, D_FF), F32).at[:FFN_CONV].set(conv_w[layer])
    act = _ffn_up(hn, w_gate, w_val, layer, cw, _row(conv_b[layer]))
    return _mm_res([act], [(w_down.astype(BF16), layer)], h, gpost, gnext, RES_ROW_TILES_FFN,
                   f"ffn_down{layer}")


def kernel(x, meta_tokens, mix_pre_norm, mix_post_norm, ffn_pre_norm, ffn_post_norm, ab_w_in, lru_conv_w, lru_conv_b, lru_gate_a_w, lru_gate_a_b, lru_gate_x_w, lru_gate_x_b, lru_lambda, rwkv_shift_mu, rwkv_w0, rwkv_w2, rwkv_a0, rwkv_a2, rwkv_g2, rwkv_k_k, rwkv_k_a, rwkv_r_k, rwkv_ln_w, rwkv_ln_b, ab_w_out, gla_w_in, gla_gk_up, gla_gk_b, gla_norm_w, gla_w_out, ffn_w_gate, ffn_w_val, ffn_conv_w, ffn_conv_b, ffn_w_down):
    batch, seq, _ = x.shape
    assert batch == 1 and seq % HEAD == 0
    W = RWKV_WIDTH
    head = jnp.concatenate([jnp.zeros((PAD, D_MODEL), F32), meta_tokens.astype(F32)], axis=0)
    hn = _embed(head, x[0], _row(mix_pre_norm[0]))

    u = _mm(hn, ab_w_in, ab_w_in.shape[2], MM_COL_TILE, "ab_in")
    cw = jnp.zeros((8, LRU_WIDTH), F32).at[:LRU_CONV].set(lru_conv_w[0])
    lru_out = _lru(u, cw, _row(lru_conv_b[0]),
                   _block_diag_groups(lru_gate_a_w[0]).astype(BF16), _row(lru_gate_a_b[0]),
                   _block_diag_groups(lru_gate_x_w[0]).astype(BF16), _row(lru_gate_x_b[0]),
                   _row(lru_lambda[0]))
    half = RWKV_LORA // 2
    wwa = jnp.zeros((RWKV_LORA, 2 * W), F32)
    wwa = wwa.at[:half, :W].set(rwkv_w2[0]).at[half:, W:].set(rwkv_a2[0]).astype(BF16)
    seg = jnp.kron(jnp.eye(RWKV_GROUP // RWKV_HEAD_DIM, dtype=F32),
                   jnp.ones((RWKV_HEAD_DIM, RWKV_HEAD_DIM), F32)).astype(BF16)
    mu = jnp.zeros((8, W), F32).at[:3].set(rwkv_shift_mu[0, :3 * W].reshape(3, W))
    chunk_tril = jnp.kron(jnp.eye(HEAD // RWKV_CHUNK, dtype=F32),
                          jnp.tril(jnp.ones((RWKV_CHUNK, RWKV_CHUNK), F32))).astype(BF16)
    rw_out = _rwkv(u, mu, _row(rwkv_shift_mu[0, 3 * W:]), wwa, _row(rwkv_w0[0]), _row(rwkv_a0[0]),
                   rwkv_g2[0].astype(BF16), _row(rwkv_k_k[0]), _row(rwkv_k_a[0]), _row(rwkv_r_k[0]),
                   _row(rwkv_ln_w[0]), _row(rwkv_ln_b[0]), seg, chunk_tril)
    w_out = ab_w_out[0].astype(BF16)
    h, hn = _mm_res([lru_out, rw_out], [w_out[:LRU_WIDTH], w_out[LRU_WIDTH:]], (x[0], head),
                    _row(mix_post_norm[0]), _row(ffn_pre_norm[0]), (HEAD,), "ab_out")
    h, hn = _ffn(h, hn, 0, ffn_w_gate, ffn_w_val, ffn_conv_w, ffn_conv_b, ffn_w_down,
                 _row(ffn_post_norm[0]), _row(mix_pre_norm[1]))

    ug = _mm(hn, jnp.swapaxes(gla_w_in, 1, 2), GLA_MAIN, MM_COL_TILE, "gla_in", transposed=True)
    w_gd = jnp.zeros((D_MODEL, 128), F32).at[:, :GLA_GATE_RANK].set(gla_w_in[0, :, GLA_MAIN:]).astype(BF16)
    up = jnp.zeros((128, GLA_KW), F32).at[:GLA_GATE_RANK].set(gla_gk_up[0]).astype(BF16)
    tril = jnp.tril(jnp.ones((GLA_CHUNK, GLA_CHUNK), F32)).astype(BF16)
    ones = jnp.ones((GLA_DK, 128), BF16)
    o = _gla(ug, hn, w_gd, up, _row(gla_gk_b[0]), _row(gla_norm_w[0]), tril, ones)
    h, hn = _mm_res([o], [gla_w_out[0].astype(BF16)], h,
                    _row(mix_post_norm[1]), _row(ffn_pre_norm[1]), RES_ROW_TILES_MIX, "gla_out")
    out, _ = _ffn(h, hn, 1, ffn_w_gate, ffn_w_val, ffn_conv_w, ffn_conv_b, ffn_w_down,
                  _row(ffn_post_norm[1]), None)
    return out[None]
```
